```python
import jax
import jax.numpy as jnp
from jax import lax
import numpy as np

D_MODEL = 2048
BATCH = 8
SEQ = 2048
DEPTH = 2
DEC_BATCH = 128
DEC_SEQ = 8
PAST_LEN = 2048
PAGE_SIZE = 128

HEAD_DIM = 128
RET_HEADS = 8
MOBA_HEADS = 8
RET_WIDTH = RET_HEADS * HEAD_DIM
MOBA_WIDTH = MOBA_HEADS * HEAD_DIM
MOBA_BLOCK = 256
MOBA_TOPK = 3
MOBA_QBLOCK = 64
GLA_HEADS = 4
GLA_DK = D_MODEL // 2 // GLA_HEADS
GLA_DV = D_MODEL // GLA_HEADS
GLA_GATE_RANK = 16
GLA_TAU = 16.0
CHUNK = 64
MEM_LEN = 256
MEM_HEADS = 4
MEM_HEAD_DIM = D_MODEL // MEM_HEADS
D_FF = -(-8 * D_MODEL // (3 * 256)) * 256
ROPE_THETA = 10000.0
DEEPNORM_ALPHA = (2 * DEPTH) ** 0.25
DEEPNORM_BETA = (8 * DEPTH) ** -0.25
LN_EPS = 1e-5

kernel_name = "retnet_moba_gla_memory_deepnorm_step"


def _layer_norm(x, g, b):
    xf = x.astype(jnp.float32)
    mu = jnp.mean(xf, axis=-1, keepdims=True)
    var = jnp.mean(jnp.square(xf - mu), axis=-1, keepdims=True)
    return ((xf - mu) * lax.rsqrt(var + LN_EPS) * g.astype(jnp.float32) + b.astype(jnp.float32)).astype(x.dtype)


def _head_norm(x):
    xf = x.astype(jnp.float32)
    mu = jnp.mean(xf, axis=-1, keepdims=True)
    var = jnp.mean(jnp.square(xf - mu), axis=-1, keepdims=True)
    return (xf - mu) * lax.rsqrt(var + LN_EPS)


def _rope(x, pos):
    half = x.shape[-1] // 2
    inv_freq = jnp.power(ROPE_THETA, -jnp.arange(half, dtype=jnp.float32) / half)
    ang = pos.astype(jnp.float32)[:, None] * inv_freq[None, :]
    cos = jnp.cos(ang)[None, :, None, :]
    sin = jnp.sin(ang)[None, :, None, :]
    xf = x.astype(jnp.float32)
    x1, x2 = xf[..., :half], xf[..., half:]
    return jnp.concatenate([x1 * cos - x2 * sin, x1 * sin + x2 * cos], axis=-1).astype(x.dtype)


def _gated_linear_recurrence(q, k, v, log_a, s0):
    B, L, H, DK = q.shape
    DV = v.shape[-1]
    C = CHUNK if L % CHUNK == 0 else L
    n = L // C

    def to_chunks(t):
        return t.astype(jnp.float32).reshape(B, n, C, H, t.shape[-1]).transpose(1, 0, 3, 2, 4)

    causal = jnp.tril(jnp.ones((C, C), dtype=bool))

    def step(S, inp):
        qc, kc, vc, gc = inp
        b = jnp.cumsum(gc, axis=2)
        o_inter = jnp.einsum('bhcd,bhde->bhce', qc * jnp.exp(b), S)
        diff = b[:, :, :, None, :] - b[:, :, None, :, :]
        decay = jnp.exp(jnp.where(causal[None, None, :, :, None], diff, -jnp.inf))
        att = jnp.einsum('bhid,bhjd,bhijd->bhij', qc, kc, decay)
        o = o_inter + jnp.einsum('bhij,bhje->bhie', att, vc)
        b_last = b[:, :, -1:, :]
        S = jnp.exp(b_last[:, :, 0, :])[..., None] * S + jnp.einsum('bhcd,bhce->bhde', kc * jnp.exp(b_last - b), vc)
        return S, o

    S, o = lax.scan(step, s0.astype(jnp.float32), (to_chunks(q), to_chunks(k), to_chunks(v), to_chunks(log_a)))
    o = o.transpose(1, 0, 3, 2, 4).reshape(B, L, H, DV)
    return o, S.astype(s0.dtype)


def _moba_chunk(q, q_start, kbh, vbh, kmean):
    Q, H, Dh = q.shape
    NB = kbh.shape[1]
    scale = Dh ** -0.5
    q_blk = q_start // MOBA_BLOCK
    q_pos = q_start + jnp.arange(Q)
    qf = q.astype(jnp.float32)
    k_own = lax.dynamic_index_in_dim(kbh, q_blk, axis=1, keepdims=False).astype(jnp.float32)
    v_own = lax.dynamic_index_in_dim(vbh, q_blk, axis=1, keepdims=False).astype(jnp.float32)
    own_pos = q_blk * MOBA_BLOCK + jnp.arange(MOBA_BLOCK)
    s_own = jnp.einsum('qhd,hkd->qhk', qf, k_own) * scale
    s_own = jnp.where((own_pos[None, :] <= q_pos[:, None])[:, None, :], s_own, -jnp.inf)
    n_sel = min(MOBA_TOPK, NB - 1)
    if n_sel == 0:
        p_own = jax.nn.softmax(s_own, axis=-1)
        return jnp.einsum('qhk,hkd->qhd', p_own, v_own).astype(q.dtype)
    gate = jnp.einsum('qhd,hnd->qhn', qf, kmean.astype(jnp.float32))
    gate = jnp.where(jnp.arange(NB) < q_blk, gate, -jnp.inf)
    _, idx = lax.top_k(gate, n_sel)
    valid = idx < q_blk
    h_ix = jnp.arange(H)[None, :, None]
    k_sel = kbh[h_ix, idx].astype(jnp.float32)
    v_sel = vbh[h_ix, idx].astype(jnp.float32)
    s_sel = jnp.einsum('qhd,qhnkd->qhnk', qf, k_sel) * scale
    s_sel = jnp.where(valid[..., None], s_sel, -jnp.inf).reshape(Q, H, n_sel * MOBA_BLOCK)
    p = jax.nn.softmax(jnp.concatenate([s_sel, s_own], axis=-1), axis=-1)
    p_sel = p[..., :n_sel * MOBA_BLOCK].reshape(Q, H, n_sel, MOBA_BLOCK)
    p_own = p[..., n_sel * MOBA_BLOCK:]
    o = jnp.einsum('qhnk,qhnkd->qhd', p_sel, v_sel) + jnp.einsum('qhk,hkd->qhd', p_own, v_own)
    return o.astype(q.dtype)


def _to_blocks(t, nb):
    L, H, Dh = t.shape
    t = jnp.pad(t, ((0, nb * MOBA_BLOCK - L), (0, 0), (0, 0)))
    return t.reshape(nb, MOBA_BLOCK, H, Dh).transpose(2, 0, 1, 3)


def _moba_prompt(q, k, v):
    B, L, H, Dh = q.shape
    nb = -(-L // MOBA_BLOCK)
    nq = L // MOBA_QBLOCK

    def per_seq(args):
        qs, ks, vs = args
        kbh, vbh = _to_blocks(ks, nb), _to_blocks(vs, nb)
        kmean = jnp.mean(kbh.astype(jnp.float32), axis=2)
        starts = jnp.arange(nq, dtype=jnp.int32) * MOBA_QBLOCK
        o = lax.map(lambda a: _moba_chunk(a[0], a[1], kbh, vbh, kmean),
                    (qs.reshape(nq, MOBA_QBLOCK, H, Dh), starts))
        return o.reshape(L, H, Dh)

    return lax.map(per_seq, (q, k, v))


def _moba_sample(q, k, v, cache_k, cache_v, page_table):
    Bs, T, H, Dh = q.shape
    past = page_table.shape[1] * PAGE_SIZE
    nb = -(-(past + T) // MOBA_BLOCK)

    def per_seq(args):
        qs, ks, vs, pt = args
        k_all = jnp.concatenate([cache_k[pt].reshape(past, H, Dh).astype(ks.dtype), ks], axis=0)
        v_all = jnp.concatenate([cache_v[pt].reshape(past, H, Dh).astype(vs.dtype), vs], axis=0)
        kbh, vbh = _to_blocks(k_all, nb), _to_blocks(v_all, nb)
        kmean = jnp.mean(kbh.astype(jnp.float32), axis=2)
        return _moba_chunk(qs, past, kbh, vbh, kmean)

    return lax.map(per_seq, (q, k, v, page_table))


def _even_mixer(x, pos, ret_s0, w_in, w_out, moba_fn):
    B, L, _ = x.shape
    cuts = [RET_WIDTH, 2 * RET_WIDTH, 3 * RET_WIDTH, 4 * RET_WIDTH,
            4 * RET_WIDTH + MOBA_WIDTH, 4 * RET_WIDTH + 2 * MOBA_WIDTH]
    rq, rk, rv, rg, mq, mk, mv = jnp.split(x @ w_in, cuts, axis=-1)
    rq = _rope(rq.reshape(B, L, RET_HEADS, HEAD_DIM), pos)
    rk = _rope(rk.reshape(B, L, RET_HEADS, HEAD_DIM), pos) * (HEAD_DIM ** -0.5)
    rv = rv.reshape(B, L, RET_HEADS, HEAD_DIM)
    log_gamma = jnp.log1p(-jnp.power(2.0, -5.0 - jnp.arange(RET_HEADS, dtype=jnp.float32)))
    log_a = jnp.broadcast_to(log_gamma[None, None, :, None], (B, L, RET_HEADS, HEAD_DIM))
    ret_o, ret_s = _gated_linear_recurrence(rq, rk, rv, log_a, ret_s0)
    ret_o = (_head_norm(ret_o).reshape(B, L, RET_WIDTH) * jax.nn.silu(rg.astype(jnp.float32))).astype(x.dtype)
    mq = _rope(mq.reshape(B, L, MOBA_HEADS, HEAD_DIM), pos)
    mk = _rope(mk.reshape(B, L, MOBA_HEADS, HEAD_DIM), pos)
    mv = mv.reshape(B, L, MOBA_HEADS, HEAD_DIM)
    moba_o = moba_fn(mq, mk, mv).reshape(B, L, MOBA_WIDTH)
    out = jnp.concatenate([ret_o, moba_o], axis=-1) @ w_out
    return out, ret_s, mk, mv


def _odd_mixer(x, gla_s0, w_in, w_gate, b_gate, w_out):
    B, L, _ = x.shape
    qk_w = GLA_HEADS * GLA_DK
    cuts = [qk_w, 2 * qk_w, 2 * qk_w + D_MODEL, 2 * qk_w + 2 * D_MODEL]
    q, k, v, r, z = jnp.split(x @ w_in, cuts, axis=-1)
    log_a = jax.nn.log_sigmoid((z @ w_gate + b_gate).astype(jnp.float32)) / GLA_TAU
    q = q.reshape(B, L, GLA_HEADS, GLA_DK)
    k = k.reshape(B, L, GLA_HEADS, GLA_DK) * (GLA_DK ** -0.5)
    v = v.reshape(B, L, GLA_HEADS, GLA_DV)
    log_a = log_a.reshape(B, L, GLA_HEADS, GLA_DK)
    o, s = _gated_linear_recurrence(q, k, v, log_a, gla_s0)
    o = (_head_norm(o).reshape(B, L, D_MODEL) * jax.nn.silu(r.astype(jnp.float32))).astype(x.dtype)
    return o @ w_out, s


def _memory_attention(x, mem_k, mem_v, w_q, w_o):
    B, L, _ = x.shape
    q = (x @ w_q).reshape(B, L, MEM_HEADS, MEM_HEAD_DIM).astype(jnp.float32)
    s = jnp.einsum('blhd,bmhd->bhlm', q, mem_k.astype(jnp.float32)) * (MEM_HEAD_DIM ** -0.5)
    p = jax.nn.softmax(s, axis=-1)
    o = jnp.einsum('bhlm,bmhd->blhd', p, mem_v.astype(jnp.float32)).astype(x.dtype).reshape(B, L, D_MODEL)
    return o @ w_o


def _swiglu(x, w_gu, w_down):
    g, u = jnp.split(x @ w_gu, 2, axis=-1)
    return (jax.nn.silu(g) * u) @ w_down


def _trunk(x, pos, mem_k, mem_v, ret_s0, gla_s0, moba_fn, w_in_a, w_out_a, w_in_c, w_gate_c, b_gate_c,
           w_out_c, w_mem_q, w_mem_o, w_ffn_gu, w_ffn_down, ln_g, ln_b):
    for l in range(DEPTH):
        if l % 2 == 0:
            h, ret_s, mk, mv = _even_mixer(x, pos, ret_s0, w_in_a, w_out_a, moba_fn)
        else:
            h, gla_s = _odd_mixer(x, gla_s0, w_in_c, w_gate_c, b_gate_c, w_out_c)
        x = _layer_norm(DEEPNORM_ALPHA * x + h, ln_g[l, 0], ln_b[l, 0])
        h = _memory_attention(x, mem_k[l], mem_v[l], w_mem_q[l], w_mem_o[l])
        x = _layer_norm(DEEPNORM_ALPHA * x + h, ln_g[l, 1], ln_b[l, 1])
        h = _swiglu(x, w_ffn_gu[l], w_ffn_down[l])
        x = _layer_norm(DEEPNORM_ALPHA * x + h, ln_g[l, 2], ln_b[l, 2])
    return x, ret_s, mk, mv, gla_s


def setup_inputs(seed: int = 0) -> dict:
    key = jax.random.key(seed)
    keys = iter(jax.random.split(key, 32))
    f32 = jnp.float32

    def nrm(shape, scale):
        return jax.random.normal(next(keys), shape, f32) * scale

    n_pages = PAST_LEN // PAGE_SIZE
    n_used = DEC_BATCH * n_pages
    n_pool = n_used + max(1, n_used // 4)
    in_a = 4 * RET_WIDTH + 3 * MOBA_WIDTH
    qk_w = GLA_HEADS * GLA_DK
    in_c = 2 * qk_w + 2 * D_MODEL + GLA_GATE_RANK

    x_prompt = nrm((BATCH, SEQ, D_MODEL), 1.0)
    x_sample = nrm((DEC_BATCH, DEC_SEQ, D_MODEL), 1.0)
    mem_prompt = nrm((BATCH, MEM_LEN, D_MODEL), 1.0)
    state_ret = nrm((DEC_BATCH, RET_HEADS, HEAD_DIM, HEAD_DIM), 0.5)
    cache_moba_k = nrm((n_pool, PAGE_SIZE, MOBA_HEADS, HEAD_DIM), 1.0)
    cache_moba_v = nrm((n_pool, PAGE_SIZE, MOBA_HEADS, HEAD_DIM), DEEPNORM_BETA)
    page_table = jax.random.permutation(next(keys), n_pool)[:n_used].reshape(DEC_BATCH, n_pages).astype(jnp.int32)
    state_gla = nrm((DEC_BATCH, GLA_HEADS, GLA_DK, GLA_DV), 0.5)
    cache_mem_k = nrm((DEPTH, DEC_BATCH, MEM_LEN, MEM_HEADS, MEM_HEAD_DIM), 1.0)
    cache_mem_v = nrm((DEPTH, DEC_BATCH, MEM_LEN, MEM_HEADS, MEM_HEAD_DIM), DEEPNORM_BETA)

    col_a = np.ones(in_a, np.float32)
    col_a[2 * RET_WIDTH:3 * RET_WIDTH] = DEEPNORM_BETA
    col_a[4 * RET_WIDTH + 2 * MOBA_WIDTH:] = DEEPNORM_BETA
    w_in_a = nrm((D_MODEL, in_a), D_MODEL ** -0.5) * col_a
    w_out_a = nrm((RET_WIDTH + MOBA_WIDTH, D_MODEL), (RET_WIDTH + MOBA_WIDTH) ** -0.5 * DEEPNORM_BETA)
    col_c = np.ones(in_c, np.float32)
    col_c[2 * qk_w:2 * qk_w + D_MODEL] = DEEPNORM_BETA
    w_in_c = nrm((D_MODEL, in_c), D_MODEL ** -0.5) * col_c
    w_gate_c = nrm((GLA_GATE_RANK, qk_w), GLA_GATE_RANK ** -0.5)
    b_gate_c = nrm((qk_w,), 0.02)
    w_out_c = nrm((D_MODEL, D_MODEL), D_MODEL ** -0.5 * DEEPNORM_BETA)
    w_mem_q = nrm((DEPTH, D_MODEL, D_MODEL), D_MODEL ** -0.5)
    w_mem_k = nrm((DEPTH, D_MODEL, D_MODEL), D_MODEL ** -0.5)
    w_mem_v = nrm((DEPTH, D_MODEL, D_MODEL), D_MODEL ** -0.5 * DEEPNORM_BETA)
    w_mem_o = nrm((DEPTH, D_MODEL, D_MODEL), D_MODEL ** -0.5 * DEEPNORM_BETA)
    w_ffn_gu = nrm((DEPTH, D_MODEL, 2 * D_FF), D_MODEL ** -0.5)
    w_ffn_down = nrm((DEPTH, D_FF, D_MODEL), D_FF ** -0.5 * DEEPNORM_BETA)
    ln_g = 1.0 + nrm((DEPTH, 3, D_MODEL), 0.02)
    ln_b = nrm((DEPTH, 3, D_MODEL), 0.02)
    return {"x_prompt": x_prompt, "x_sample": x_sample, "mem_prompt": mem_prompt,
            "state_ret": state_ret, "cache_moba_k": cache_moba_k, "cache_moba_v": cache_moba_v,
            "page_table": page_table, "state_gla": state_gla,
            "cache_mem_k": cache_mem_k, "cache_mem_v": cache_mem_v,
            "w_in_a": w_in_a, "w_out_a": w_out_a, "w_in_c": w_in_c, "w_gate_c": w_gate_c,
            "b_gate_c": b_gate_c, "w_out_c": w_out_c, "w_mem_q": w_mem_q, "w_mem_k": w_mem_k,
            "w_mem_v": w_mem_v, "w_mem_o": w_mem_o, "w_ffn_gu": w_ffn_gu, "w_ffn_down": w_ffn_down,
            "ln_g": ln_g, "ln_b": ln_b}


def reference(x_prompt, x_sample, mem_prompt, state_ret, cache_moba_k, cache_moba_v, page_table, state_gla,
              cache_mem_k, cache_mem_v, w_in_a, w_out_a, w_in_c, w_gate_c, b_gate_c, w_out_c,
              w_mem_q, w_mem_k, w_mem_v, w_mem_o, w_ffn_gu, w_ffn_down, ln_g, ln_b):
    B, L, _ = x_prompt.shape
    T = x_sample.shape[1]
    m_len = mem_prompt.shape[1]
    past = page_table.shape[1] * PAGE_SIZE
    pos_p = jnp.arange(L, dtype=jnp.int32)
    pos_s = past + jnp.arange(T, dtype=jnp.int32)

    mem_k_p = jnp.einsum('bmd,lde->lbme', mem_prompt, w_mem_k).reshape(DEPTH, B, m_len, MEM_HEADS, MEM_HEAD_DIM)
    mem_v_p = jnp.einsum('bmd,lde->lbme', mem_prompt, w_mem_v).reshape(DEPTH, B, m_len, MEM_HEADS, MEM_HEAD_DIM)
    ret0_p = jnp.zeros((B, RET_HEADS, HEAD_DIM, HEAD_DIM), jnp.float32)
    gla0_p = jnp.zeros((B, GLA_HEADS, GLA_DK, GLA_DV), jnp.float32)
    y_p, ret_p, mk_p, mv_p, gla_p = _trunk(
        x_prompt, pos_p, mem_k_p, mem_v_p, ret0_p, gla0_p, _moba_prompt,
        w_in_a, w_out_a, w_in_c, w_gate_c, b_gate_c, w_out_c, w_mem_q, w_mem_o, w_ffn_gu, w_ffn_down, ln_g, ln_b)

    moba_s = lambda q, k, v: _moba_sample(q, k, v, cache_moba_k, cache_moba_v, page_table)
    y_s, ret_s, mk_s, mv_s, gla_s = _trunk(
        x_sample, pos_s, cache_mem_k, cache_mem_v, state_ret, state_gla, moba_s,
        w_in_a, w_out_a, w_in_c, w_gate_c, b_gate_c, w_out_c, w_mem_q, w_mem_o, w_ffn_gu, w_ffn_down, ln_g, ln_b)

    return (y_p, y_s, ret_p, mk_p, mv_p, gla_p, mem_k_p, mem_v_p, ret_s, mk_s, mv_s, gla_s)
```

```python
import functools

import jax
import jax.numpy as jnp
import numpy as np
from jax import lax
from jax.experimental import pallas as pl
from jax.experimental.pallas import tpu as pltpu

F32 = jnp.float32
BF16 = jnp.bfloat16

HEAD_DIM = 128
RET_HEADS = 8
MOBA_HEADS = 8
MOBA_BLOCK = 256
MOBA_TOPK = 3
GLA_HEADS = 4
GLA_GATE_RANK = 16
GLA_TAU = 16.0
MEM_HEADS = 4
PAGE_SIZE = 128
ROPE_THETA = 10000.0
LN_EPS = 1e-5

LANES = 128
V7X_VMEM_BYTES = 64 * 2**20
VMEM_LIMIT = 52 * 2**20

NEG = -1e30
EXP_CLAMP = 80.0


def _cp(*sem):
    return pltpu.CompilerParams(dimension_semantics=sem, vmem_limit_bytes=VMEM_LIMIT)


def _dot(a, b):
    return jnp.dot(a, b, preferred_element_type=F32)


def _dot_nt(a, b):
    return lax.dot_general(a, b, (((1,), (1,)), ((), ())), preferred_element_type=F32)


def _dot_tn(a_f32, b):
    return jnp.dot(a_f32.T.astype(BF16), b, preferred_element_type=F32)


def _row_tile(m, pref):
    t = min(m, pref)
    while m % t:
        t -= 8
    assert t > 0 and t % 8 == 0
    return t


def _silu(x):
    return x * jax.nn.sigmoid(x)


def _head_norm(o):
    mu = jnp.mean(o, axis=-1, keepdims=True)
    d = o - mu
    var = jnp.mean(d * d, axis=-1, keepdims=True)
    return d * lax.rsqrt(var + LN_EPS)


def _ln_store(y, g_ref, b_ref, o_ref, ob_ref):
    mu = jnp.mean(y, axis=-1, keepdims=True)
    d = y - mu
    var = jnp.mean(d * d, axis=-1, keepdims=True)
    out = d * lax.rsqrt(var + LN_EPS) * g_ref[...] + b_ref[...]
    o_ref[...] = out
    ob_ref[...] = out.astype(BF16)


def _proj_kernel(*refs, modes, tn):
    use_rope = any(m[0] for m in modes)
    if use_rope:
        x_ref, w_ref, cos_ref, sin_ref, o_ref = refs
    else:
        x_ref, w_ref, o_ref = refs
    acc = _dot(x_ref[...].astype(BF16), w_ref[...])
    j = pl.program_id(1)

    def store(mode):
        do_rope, scale = mode
        if not do_rope:
            v = acc if scale == 1.0 else acc * scale
            o_ref[...] = v.astype(o_ref.dtype)
            return
        c = cos_ref[...]
        s = sin_ref[...]
        for h in range(tn // HEAD_DIM):
            sl = slice(h * HEAD_DIM, (h + 1) * HEAD_DIM)
            a = acc[:, sl]
            r = a * c + pltpu.roll(a, HEAD_DIM // 2, 1) * s
            if scale != 1.0:
                r = r * scale
            o_ref[:, sl] = r.astype(o_ref.dtype)

    distinct = sorted(set(modes))
    if len(distinct) == 1:
        store(distinct[0])
        return
    for mode in distinct:
        pred = None
        for jj, m in enumerate(modes):
            if m == mode:
                p = j == jj
                pred = p if pred is None else jnp.logical_or(pred, p)
        pl.when(pred)(functools.partial(store, mode))


def _proj(x, w, out_dtype, modes=None, tn=1024, rope=None, tm_pref=1024, name="proj"):
    m, k = x.shape
    n = w.shape[1]
    tn = min(tn, n)
    assert n % tn == 0
    nt = n // tn
    modes = tuple(modes) if modes is not None else ((False, 1.0),) * nt
    assert len(modes) == nt
    tm = _row_tile(m, tm_pref)
    in_specs = [pl.BlockSpec((tm, k), lambda i, j: (i, 0)),
                pl.BlockSpec((k, tn), lambda i, j: (0, j))]
    args = [x, w]
    if any(mm[0] for mm in modes):
        cos, sin = rope
        p_rows = cos.shape[0]
        assert p_rows % tm == 0
        nblk = p_rows // tm
        in_specs += [pl.BlockSpec((tm, HEAD_DIM), lambda i, j: (i % nblk, 0))] * 2
        args += [cos, sin]
    return pl.pallas_call(
        functools.partial(_proj_kernel, modes=modes, tn=tn),
        out_shape=jax.ShapeDtypeStruct((m, n), out_dtype),
        grid=(m // tm, nt),
        in_specs=in_specs,
        out_specs=pl.BlockSpec((tm, tn), lambda i, j: (i, j)),
        compiler_params=_cp("parallel", "arbitrary"),
        name=name,
    )(*args)


def _proj_ln_kernel(*refs, n_in, alpha):
    a_refs = refs[:n_in]
    w_refs = refs[n_in:2 * n_in]
    x_ref, g_ref, b_ref, o_ref, ob_ref = refs[2 * n_in:]
    acc = None
    for a_ref, w_ref in zip(a_refs, w_refs):
        d = _dot(a_ref[...].astype(BF16), w_ref[...])
        acc = d if acc is None else acc + d
    _ln_store(alpha * x_ref[...] + acc, g_ref, b_ref, o_ref, ob_ref)


def _proj_ln(a_list, w_list, x, g, b, alpha, name="proj_ln"):
    m, d = x.shape
    tm = _row_tile(m, 512)
    in_specs = [pl.BlockSpec((tm, a.shape[1]), lambda i: (i, 0)) for a in a_list]
    in_specs += [pl.BlockSpec(w.shape, lambda i: (0, 0), pipeline_mode=pl.Buffered(1)) for w in w_list]
    in_specs += [pl.BlockSpec((tm, d), lambda i: (i, 0)),
                 pl.BlockSpec((1, d), lambda i: (0, 0)),
                 pl.BlockSpec((1, d), lambda i: (0, 0))]
    return pl.pallas_call(
        functools.partial(_proj_ln_kernel, n_in=len(a_list), alpha=alpha),
        out_shape=(jax.ShapeDtypeStruct((m, d), F32), jax.ShapeDtypeStruct((m, d), BF16)),
        grid=(m // tm,),
        in_specs=in_specs,
        out_specs=(pl.BlockSpec((tm, d), lambda i: (i, 0)), pl.BlockSpec((tm, d), lambda i: (i, 0))),
        compiler_params=_cp("parallel"),
        name=name,
    )(*a_list, *w_list, x, g.reshape(1, d), b.reshape(1, d))


def _ffn_kernel(xb_ref, x_ref, wg_ref, wu_ref, wd_ref, g_ref, b_ref, o_ref, ob_ref, acc_ref, *, alpha):
    f = pl.program_id(1)

    @pl.when(f == 0)
    def _():
        acc_ref[...] = jnp.zeros_like(acc_ref)

    xb = xb_ref[...]
    gate = _dot(xb, wg_ref[...])
    up = _dot(xb, wu_ref[...])
    hid = (_silu(gate) * up).astype(BF16)
    acc_ref[...] += _dot(hid, wd_ref[...])

    @pl.when(f == pl.num_programs(1) - 1)
    def _():
        _ln_store(alpha * x_ref[...] + acc_ref[...], g_ref, b_ref, o_ref, ob_ref)


def _ffn(xb, x, w_gu, w_down, g, b, alpha, name="ffn"):
    m, d = x.shape
    ff = w_down.shape[0]
    tm = _row_tile(m, 512)
    tf = 512 if ff % 512 == 0 else 256
    assert ff % tf == 0
    nf = ff // tf
    return pl.pallas_call(
        functools.partial(_ffn_kernel, alpha=alpha),
        out_shape=(jax.ShapeDtypeStruct((m, d), F32), jax.ShapeDtypeStruct((m, d), BF16)),
        grid=(m // tm, nf),
        in_specs=[pl.BlockSpec((tm, d), lambda i, f: (i, 0)),
                  pl.BlockSpec((tm, d), lambda i, f: (i, 0)),
                  pl.BlockSpec((d, tf), lambda i, f: (0, f)),
                  pl.BlockSpec((d, tf), lambda i, f: (0, nf + f)),
                  pl.BlockSpec((tf, d), lambda i, f: (f, 0)),
                  pl.BlockSpec((1, d), lambda i, f: (0, 0)),
                  pl.BlockSpec((1, d), lambda i, f: (0, 0))],
        out_specs=(pl.BlockSpec((tm, d), lambda i, f: (i, 0)), pl.BlockSpec((tm, d), lambda i, f: (i, 0))),
        scratch_shapes=[pltpu.VMEM((tm, d), F32)],
        compiler_params=_cp("parallel", "arbitrary"),
        name=name,
    )(xb, x, w_gu, w_gu, w_down, g.reshape(1, d), b.reshape(1, d))


def _ret_kernel(q_ref, k_ref, v_ref, g_ref, s0_ref, dmat_ref, qdec_ref, kdec_ref, gc_ref,
                o_ref, sout_ref, s_scr):
    c = pl.program_id(2)

    @pl.when(c == 0)
    def _():
        s_scr[...] = s0_ref[0, 0]

    q = q_ref[...].astype(F32)
    k = k_ref[...].astype(F32)
    v = v_ref[...].astype(BF16)
    s_prev = s_scr[...]
    att = _dot_nt(q.astype(BF16), k.astype(BF16)) * dmat_ref[0]
    o = _dot(att.astype(BF16), v) + _dot((q * qdec_ref[0]).astype(BF16), s_prev.astype(BF16))
    s_new = gc_ref[0] * s_prev + _dot_tn(k * kdec_ref[0], v)
    s_scr[...] = s_new
    o_ref[...] = (_head_norm(o) * _silu(g_ref[...].astype(F32))).astype(o_ref.dtype)

    @pl.when(c == pl.num_programs(2) - 1)
    def _():
        sout_ref[0, 0] = s_new


def _retention(arr, cols, s0, n_seq, seq_len, chunk, out_dtype, name="retention"):
    nc = seq_len // chunk
    h_n = RET_HEADS
    lg = np.log1p(-np.power(2.0, -5.0 - np.arange(h_n, dtype=np.float64)))[:, None, None]
    i = np.arange(chunk, dtype=np.float64)
    diff = i[:, None] - i[None, :]
    dmat = np.where(diff >= 0, np.exp(lg * np.maximum(diff, 0.0)), 0.0).astype(np.float32)
    qdec = np.broadcast_to(np.exp(lg * (i[None, :, None] + 1.0)), (h_n, chunk, HEAD_DIM)).astype(np.float32)
    kdec = np.broadcast_to(np.exp(lg * (chunk - 1.0 - i[None, :, None])), (h_n, chunk, HEAD_DIM)).astype(np.float32)
    gc = np.broadcast_to(np.exp(lg * chunk), (h_n, 1, HEAD_DIM)).astype(np.float32)

    def col_spec(off):
        return pl.BlockSpec((chunk, HEAD_DIM), lambda b, h, c: (b * nc + c, off + h))

    state_spec = pl.BlockSpec((1, 1, HEAD_DIM, HEAD_DIM), lambda b, h, c: (b, h, 0, 0))
    return pl.pallas_call(
        _ret_kernel,
        out_shape=(jax.ShapeDtypeStruct((n_seq * seq_len, h_n * HEAD_DIM), out_dtype),
                   jax.ShapeDtypeStruct(s0.shape, F32)),
        grid=(n_seq, h_n, nc),
        in_specs=[col_spec(cols[0]), col_spec(cols[1]), col_spec(cols[2]), col_spec(cols[3]),
                  state_spec,
                  pl.BlockSpec((1, chunk, chunk), lambda b, h, c: (h, 0, 0)),
                  pl.BlockSpec((1, chunk, HEAD_DIM), lambda b, h, c: (h, 0, 0)),
                  pl.BlockSpec((1, chunk, HEAD_DIM), lambda b, h, c: (h, 0, 0)),
                  pl.BlockSpec((1, 1, HEAD_DIM), lambda b, h, c: (h, 0, 0))],
        out_specs=(pl.BlockSpec((chunk, HEAD_DIM), lambda b, h, c: (b * nc + c, h)), state_spec),
        scratch_shapes=[pltpu.VMEM((HEAD_DIM, HEAD_DIM), F32)],
        compiler_params=_cp("parallel", "parallel", "arbitrary"),
        name=name,
    )(arr, arr, arr, arr, s0, jnp.asarray(dmat), jnp.asarray(qdec), jnp.asarray(kdec), jnp.asarray(gc))


def _moba_prompt_kernel(q_ref, k_ref, v_ref, o_ref, kmean_scr, *, nb, n_sel):
    qi = pl.program_id(2)
    blk_rows = MOBA_BLOCK
    scale = HEAD_DIM ** -0.5

    @pl.when(qi == 0)
    def _():
        for n in range(nb):
            kmean_scr[n:n + 1, :] = jnp.mean(k_ref[n * blk_rows:(n + 1) * blk_rows, :], axis=0, keepdims=True)

    q = q_ref[...].astype(BF16)
    gate = _dot_nt(q, kmean_scr[...].astype(BF16))
    blk = lax.broadcasted_iota(jnp.int32, gate.shape, 1)
    valid = blk < qi
    gate = jnp.where(valid, gate, -jnp.inf)
    rank = jnp.zeros(gate.shape, jnp.int32)
    for m in range(nb):
        gm = gate[:, m:m + 1]
        beats = jnp.where(gm > gate, 1, jnp.where(gm == gate, jnp.where(blk > m, 1, 0), 0))
        rank = rank + beats
    selb = jnp.where(valid, jnp.where(rank < n_sel, 0.0, NEG), NEG)

    row0 = pl.multiple_of(qi * blk_rows, blk_rows)
    k_own = k_ref[pl.ds(row0, blk_rows), :].astype(BF16)
    v_own = v_ref[pl.ds(row0, blk_rows), :].astype(BF16)
    s = _dot_nt(q, k_own) * scale
    r_i = lax.broadcasted_iota(jnp.int32, s.shape, 0)
    c_i = lax.broadcasted_iota(jnp.int32, s.shape, 1)
    s = jnp.where(c_i <= r_i, s, NEG)
    m0 = jnp.max(s, axis=-1, keepdims=True)
    p = jnp.exp(s - m0)
    l0 = jnp.sum(p, axis=-1, keepdims=True)
    acc0 = _dot(p.astype(BF16), v_own)

    def body(n, carry):
        m_i, l_i, acc = carry
        rn = pl.multiple_of(n * blk_rows, blk_rows)
        kn = k_ref[pl.ds(rn, blk_rows), :].astype(BF16)
        vn = v_ref[pl.ds(rn, blk_rows), :].astype(BF16)
        bias = jnp.sum(jnp.where(blk == n, selb, 0.0), axis=-1, keepdims=True)
        sn = _dot_nt(q, kn) * scale + bias
        m_new = jnp.maximum(m_i, jnp.max(sn, axis=-1, keepdims=True))
        a = jnp.exp(m_i - m_new)
        pn = jnp.exp(sn - m_new)
        l_new = a * l_i + jnp.sum(pn, axis=-1, keepdims=True)
        acc_new = a * acc + _dot(pn.astype(BF16), vn)
        return m_new, l_new, acc_new

    _, l_f, acc_f = lax.fori_loop(0, qi, body, (m0, l0, acc0))
    o_ref[...] = (acc_f / l_f).astype(o_ref.dtype)


def _moba_prompt(q_arr, q_col, k_arr, v_arr, n_seq, seq_len, name="moba_prompt"):
    assert seq_len % MOBA_BLOCK == 0
    nb = seq_len // MOBA_BLOCK
    n_sel = min(MOBA_TOPK, nb - 1)
    return pl.pallas_call(
        functools.partial(_moba_prompt_kernel, nb=nb, n_sel=n_sel),
        out_shape=jax.ShapeDtypeStruct((n_seq * seq_len, MOBA_HEADS * HEAD_DIM), BF16),
        grid=(n_seq, MOBA_HEADS, nb),
        in_specs=[pl.BlockSpec((MOBA_BLOCK, HEAD_DIM), lambda b, h, i: (b * nb + i, q_col + h)),
                  pl.BlockSpec((seq_len, HEAD_DIM), lambda b, h, i: (b, h)),
                  pl.BlockSpec((seq_len, HEAD_DIM), lambda b, h, i: (b, h))],
        out_specs=pl.BlockSpec((MOBA_BLOCK, HEAD_DIM), lambda b, h, i: (b * nb + i, h)),
        scratch_shapes=[pltpu.VMEM((nb, HEAD_DIM), F32)],
        compiler_params=_cp("parallel", "parallel", "arbitrary"),
        name=name,
    )(q_arr, k_arr, v_arr)


def _moba_sample_kernel(pt_ref, q_ref, kn_ref, vn_ref, *refs, n_pg, n_blk, n_sel):
    k_pages = refs[:n_pg]
    v_pages = refs[n_pg:2 * n_pg]
    o_ref, m_scr, l_scr, acc_scr, km_scr = refs[2 * n_pg:]
    c = pl.program_id(1)
    t_n = q_ref.shape[0]
    scale = HEAD_DIM ** -0.5
    pages_per_blk = MOBA_BLOCK // PAGE_SIZE
    blk_per_step = n_pg // pages_per_blk
    q_all = q_ref[...]

    for h in range(MOBA_HEADS):
        sl = slice(h * HEAD_DIM, (h + 1) * HEAD_DIM)
        qh = q_all[:, sl].astype(BF16)
        for j in range(blk_per_step):
            pgs = range(j * pages_per_blk, (j + 1) * pages_per_blk)
            kb = jnp.concatenate([k_pages[p][:, sl] for p in pgs], axis=0)
            vb = jnp.concatenate([v_pages[p][:, sl] for p in pgs], axis=0)
            s = _dot_nt(qh, kb.astype(BF16)) * scale
            m = jnp.max(s, axis=-1, keepdims=True)
            p_ = jnp.exp(s - m)
            l = jnp.sum(p_, axis=-1, keepdims=True)
            n_glob = c * blk_per_step + j
            acc_scr[n_glob, h] = _dot(p_.astype(BF16), vb.astype(BF16))
            m_scr[n_glob, h] = jnp.broadcast_to(m, (t_n, HEAD_DIM))
            l_scr[n_glob, h] = jnp.broadcast_to(l, (t_n, HEAD_DIM))
            km_scr[n_glob, h] = jnp.broadcast_to(jnp.mean(kb, axis=0, keepdims=True), (t_n, HEAD_DIM))

    @pl.when(c == pl.num_programs(1) - 1)
    def _():
        kn_all = kn_ref[...]
        vn_all = vn_ref[...]
        r_i = lax.broadcasted_iota(jnp.int32, (t_n, t_n), 0)
        c_i = lax.broadcasted_iota(jnp.int32, (t_n, t_n), 1)
        for h in range(MOBA_HEADS):
            sl = slice(h * HEAD_DIM, (h + 1) * HEAD_DIM)
            qf = q_all[:, sl]
            gates = [jnp.sum(qf * km_scr[n, h], axis=-1, keepdims=True) for n in range(n_blk)]
            s_own = _dot_nt(qf.astype(BF16), kn_all[:, sl].astype(BF16)) * scale
            s_own = jnp.where(c_i <= r_i, s_own, NEG)
            m_tot = jnp.max(s_own, axis=-1, keepdims=True)
            sel = []
            for n in range(n_blk):
                rank = jnp.zeros((t_n, 1), jnp.int32)
                for mm in range(n_blk):
                    if mm == n:
                        continue
                    beats = (gates[mm] >= gates[n]) if mm < n else (gates[mm] > gates[n])
                    rank = rank + jnp.where(beats, 1, 0)
                sel.append(rank < n_sel)
                m_tot = jnp.maximum(m_tot, jnp.where(sel[n], m_scr[n, h][:, :1], NEG))
            p_own = jnp.exp(s_own - m_tot)
            den = jnp.sum(p_own, axis=-1, keepdims=True)
            num = _dot(p_own.astype(BF16), vn_all[:, sl].astype(BF16))
            for n in range(n_blk):
                w = jnp.where(sel[n], jnp.exp(m_scr[n, h] - m_tot), 0.0)
                num = num + w * acc_scr[n, h]
                den = den + w[:, :1] * l_scr[n, h][:, :1]
            o_ref[:, sl] = num / den


def _moba_sample(q, kn, vn, cache_k, cache_v, page_table, name="moba_sample"):
    bs, n_pages = page_table.shape
    t_n = q.shape[0] // bs
    width = MOBA_HEADS * HEAD_DIM
    past = n_pages * PAGE_SIZE
    assert past % MOBA_BLOCK == 0 and t_n <= MOBA_BLOCK
    n_blk = past // MOBA_BLOCK
    n_sel = min(MOBA_TOPK, n_blk)
    n_pg = 8 if n_pages % 8 == 0 else MOBA_BLOCK // PAGE_SIZE
    n_steps = n_pages // n_pg
    ck = cache_k.reshape(cache_k.shape[0], PAGE_SIZE, width)
    cv = cache_v.reshape(cache_v.shape[0], PAGE_SIZE, width)

    def page_spec(p):
        return pl.BlockSpec((None, PAGE_SIZE, width), lambda b, c, pt: (pt[b * n_pages + c * n_pg + p], 0, 0))

    tok_spec = pl.BlockSpec((t_n, width), lambda b, c, pt: (b, 0))
    part = pltpu.VMEM((n_blk, MOBA_HEADS, t_n, HEAD_DIM), F32)
    grid_spec = pltpu.PrefetchScalarGridSpec(
        num_scalar_prefetch=1,
        grid=(bs, n_steps),
        in_specs=[tok_spec, tok_spec, tok_spec] + [page_spec(p) for p in range(n_pg)] * 2,
        out_specs=tok_spec,
        scratch_shapes=[part, part, part, part],
    )
    return pl.pallas_call(
        functools.partial(_moba_sample_kernel, n_pg=n_pg, n_blk=n_blk, n_sel=n_sel),
        out_shape=jax.ShapeDtypeStruct((bs * t_n, width), F32),
        grid_spec=grid_spec,
        compiler_params=_cp("parallel", "arbitrary"),
        name=name,
    )(page_table.reshape(-1), q, kn, vn, *([ck] * n_pg), *([cv] * n_pg))


def _memattn_kernel(q_ref, k_ref, v_ref, o_ref):
    d = q_ref.shape[1]
    hd = d // MEM_HEADS
    scale = hd ** -0.5
    for h in range(MEM_HEADS):
        sl = slice(h * hd, (h + 1) * hd)
        qh = q_ref[:, sl].astype(BF16)
        kh = k_ref[:, sl].astype(BF16)
        vh = v_ref[:, sl].astype(BF16)
        s = _dot_nt(qh, kh) * scale
        m = jnp.max(s, axis=-1, keepdims=True)
        p = jnp.exp(s - m)
        l = jnp.sum(p, axis=-1, keepdims=True)
        o_ref[:, sl] = (_dot(p.astype(BF16), vh) / l).astype(o_ref.dtype)


def _memattn(q, mem_k, mem_v, layer, n_seq, seq_len, mem_len, out_dtype, name="memattn"):
    d = q.shape[1]
    tm = _row_tile(seq_len, 512)
    nt = seq_len // tm
    kv_spec = pl.BlockSpec((mem_len, d), lambda b, i: (layer * n_seq + b, 0))
    return pl.pallas_call(
        _memattn_kernel,
        out_shape=jax.ShapeDtypeStruct(q.shape, out_dtype),
        grid=(n_seq, nt),
        in_specs=[pl.BlockSpec((tm, d), lambda b, i: (b * nt + i, 0)), kv_spec, kv_spec],
        out_specs=pl.BlockSpec((tm, d), lambda b, i: (b * nt + i, 0)),
        compiler_params=_cp("parallel", "arbitrary"),
        name=name,
    )(q, mem_k, mem_v)


def _gla_gate_kernel(x_ref, wz_ref, wg_ref, b_ref, o_ref):
    z = _dot(x_ref[...].astype(BF16), wz_ref[...])
    g = _dot(z.astype(BF16), wg_ref[...]) + b_ref[...]
    o_ref[...] = (jnp.minimum(g, 0.0) - jnp.log1p(jnp.exp(-jnp.abs(g)))) * (1.0 / GLA_TAU)


def _gla_gate(x, w_z, w_gate, b_gate, name="gla_gate"):
    m, d = x.shape
    n = w_gate.shape[1]
    tm = _row_tile(m, 512)
    return pl.pallas_call(
        _gla_gate_kernel,
        out_shape=jax.ShapeDtypeStruct((m, n), F32),
        grid=(m // tm,),
        in_specs=[pl.BlockSpec((tm, d), lambda i: (i, 0)),
                  pl.BlockSpec(w_z.shape, lambda i: (0, 0)),
                  pl.BlockSpec(w_gate.shape, lambda i: (0, 0)),
                  pl.BlockSpec((1, n), lambda i: (0, 0))],
        out_specs=pl.BlockSpec((tm, n), lambda i: (i, 0)),
        compiler_params=_cp("parallel"),
        name=name,
    )(x, w_z, w_gate, b_gate.reshape(1, n))


def _gla_kernel(q_ref, k_ref, v_ref, r_ref, la_ref, s0_ref, o_ref, sout_ref, s_scr):
    c = pl.program_id(2)
    chunk, dk = q_ref.shape
    dv = v_ref.shape[1]

    @pl.when(c == 0)
    def _():
        s_scr[...] = s0_ref[0, 0]

    q = q_ref[...].astype(F32)
    k = k_ref[...].astype(F32)
    v = v_ref[...].astype(BF16)
    la = la_ref[...]
    r_i = lax.broadcasted_iota(jnp.int32, (chunk, chunk), 0)
    c_i = lax.broadcasted_iota(jnp.int32, (chunk, chunk), 1)
    causal = c_i <= r_i
    tri = jnp.where(causal, 1.0, 0.0).astype(BF16)
    la_hi = la.astype(BF16)
    rem = la - la_hi.astype(F32)
    la_mid = rem.astype(BF16)
    la_lo = (rem - la_mid.astype(F32)).astype(BF16)
    b = _dot(tri, la_hi) + _dot(tri, la_mid) + _dot(tri, la_lo)
    b_last = b[chunk - 1:chunk, :]
    b_mid = b[chunk // 2 - 1:chunk // 2, :]
    qt = (q * jnp.exp(jnp.minimum(b - b_mid, EXP_CLAMP))).astype(BF16)
    kt = (k * jnp.exp(jnp.minimum(b_mid - b, EXP_CLAMP))).astype(BF16)
    att = jnp.where(causal, _dot_nt(qt, kt), 0.0)
    s_prev = s_scr[...]
    o = _dot(att.astype(BF16), v) + _dot((q * jnp.exp(b)).astype(BF16), s_prev.astype(BF16))
    upd = _dot_tn(k * jnp.exp(b_last - b), v)
    col = jnp.transpose(jnp.broadcast_to(jnp.exp(b_last), (LANES, dk)))
    for jv in range(dv // LANES):
        sl = slice(jv * LANES, (jv + 1) * LANES)
        s_scr[:, sl] = col * s_prev[:, sl] + upd[:, sl]
    o_ref[...] = (_head_norm(o) * _silu(r_ref[...].astype(F32))).astype(o_ref.dtype)

    @pl.when(c == pl.num_programs(2) - 1)
    def _():
        sout_ref[0, 0] = s_scr[...]


def _gla(arr, cols, la, s0, n_seq, seq_len, chunk, out_dtype, name="gla"):
    _, h_n, dk, dv = s0.shape
    nc = seq_len // chunk
    state_spec = pl.BlockSpec((1, 1, dk, dv), lambda b, h, c: (b, h, 0, 0))

    def col_spec(width, off):
        return pl.BlockSpec((chunk, width), lambda b, h, c: (b * nc + c, off + h))

    return pl.pallas_call(
        _gla_kernel,
        out_shape=(jax.ShapeDtypeStruct((n_seq * seq_len, h_n * dv), out_dtype),
                   jax.ShapeDtypeStruct(s0.shape, F32)),
        grid=(n_seq, h_n, nc),
        in_specs=[col_spec(dk, cols[0]), col_spec(dk, cols[1]), col_spec(dv, cols[2]), col_spec(dv, cols[3]),
                  col_spec(dk, 0), state_spec],
        out_specs=(col_spec(dv, 0), state_spec),
        scratch_shapes=[pltpu.VMEM((dk, dv), F32)],
        compiler_params=_cp("parallel", "parallel", "arbitrary"),
        name=name,
    )(arr, arr, arr, arr, la, s0)


def _rope_tables(pos):
    half = HEAD_DIM // 2
    inv_freq = jnp.power(ROPE_THETA, -jnp.arange(half, dtype=F32) / half)
    ang = pos.astype(F32)[:, None] * inv_freq[None, :]
    cos, sin = jnp.cos(ang), jnp.sin(ang)
    return jnp.concatenate([cos, cos], axis=-1), jnp.concatenate([-sin, sin], axis=-1)


def _layer_tail(x, xb, layer, mem_k, mem_v, n_seq, seq_len, per_seq_dtype, wts, alpha, tag):
    ln_g, ln_b = wts["ln_g"], wts["ln_b"]
    mem_len = mem_k.shape[0] // (wts["depth"] * n_seq)
    q = _proj(xb, wts["w_mem_q"][layer], per_seq_dtype, name=f"memq_{tag}")
    o = _memattn(q, mem_k, mem_v, layer, n_seq, seq_len, mem_len, per_seq_dtype, name=f"memattn_{tag}")
    x, xb = _proj_ln([o], [wts["w_mem_o"][layer]], x, ln_g[layer, 1], ln_b[layer, 1], alpha, name=f"memo_{tag}")
    return _ffn(xb, x, wts["w_ffn_gu"][layer], wts["w_ffn_down"][layer], ln_g[layer, 2], ln_b[layer, 2], alpha,
                name=f"ffn_{tag}")


def kernel(x_prompt, x_sample, mem_prompt, state_ret, cache_moba_k, cache_moba_v, page_table, state_gla,
           cache_mem_k, cache_mem_v, w_in_a, w_out_a, w_in_c, w_gate_c, b_gate_c, w_out_c,
           w_mem_q, w_mem_k, w_mem_v, w_mem_o, w_ffn_gu, w_ffn_down, ln_g, ln_b):
    n_b, seq, d = x_prompt.shape
    n_bs, t_n, _ = x_sample.shape
    mem_len = mem_prompt.shape[1]
    depth = w_mem_q.shape[0]
    assert depth == 2
    alpha = float((2 * depth) ** 0.25)
    past = page_table.shape[1] * PAGE_SIZE
    n_p, n_s = n_b * seq, n_bs * t_n
    rw = RET_HEADS * HEAD_DIM
    mw = MOBA_HEADS * HEAD_DIM
    _, gh, dk, dv = state_gla.shape
    qk_w = gh * dk
    hb = HEAD_DIM
    k_scale = HEAD_DIM ** -0.5

    wts = {
        "depth": depth, "ln_g": ln_g, "ln_b": ln_b,
        "w_mem_q": w_mem_q.astype(BF16), "w_mem_o": w_mem_o.astype(BF16),
        "w_ffn_gu": w_ffn_gu.astype(BF16), "w_ffn_down": w_ffn_down.astype(BF16),
    }
    w_in_a_b = w_in_a.astype(BF16)
    w_out_a_b = w_out_a.astype(BF16)
    w_in_c_b = w_in_c.astype(BF16)
    w_out_c_b = w_out_c.astype(BF16)
    w_main_c = w_in_c_b[:, :2 * qk_w + 2 * d]
    w_z = w_in_c_b[:, 2 * qk_w + 2 * d:]
    w_gate_b = w_gate_c.astype(BF16)
    gla_modes = [(False, 1.0)] * ((2 * qk_w + 2 * d) // 1024)
    assert qk_w == 1024
    gla_modes[1] = (False, float(dk ** -0.5))

    rope_p = _rope_tables(jnp.arange(seq, dtype=jnp.int32))
    xp = x_prompt.reshape(n_p, d)
    xpb = xp.astype(BF16)
    mem_b = mem_prompt.reshape(n_b * mem_len, d).astype(BF16)
    mem_k_p = jnp.stack([_proj(mem_b, w_mem_k[l].astype(BF16), F32, name="memk_p") for l in range(depth)])
    mem_v_p = jnp.stack([_proj(mem_b, w_mem_v[l].astype(BF16), F32, name="memv_p") for l in range(depth)])

    a_modes = [(True, 1.0), (True, k_scale), (False, 1.0), (False, 1.0), (True, 1.0)]
    pa = _proj(xpb, w_in_a_b[:, :4 * rw + mw], BF16, a_modes, rope=rope_p, name="in_a_p")
    mk_p = _proj(xpb, w_in_a_b[:, 4 * rw + mw:4 * rw + 2 * mw], F32, [(True, 1.0)], rope=rope_p, name="in_mk_p")
    mv_p = _proj(xpb, w_in_a_b[:, 4 * rw + 2 * mw:], F32, name="in_mv_p")
    ret0 = jnp.zeros((n_b, RET_HEADS, HEAD_DIM, HEAD_DIM), F32)
    ret_chunk = 256 if seq % 256 == 0 else seq
    ret_o, ret_p = _retention(pa, [0, rw // hb, 2 * rw // hb, 3 * rw // hb], ret0, n_b, seq, ret_chunk, BF16,
                              name="retention_p")
    moba_o = _moba_prompt(pa, 4 * rw // hb, mk_p, mv_p, n_b, seq)
    x, xb = _proj_ln([ret_o, moba_o], [w_out_a_b[:rw], w_out_a_b[rw:]], xp, ln_g[0, 0], ln_b[0, 0], alpha,
                     name="out_a_p")
    mkp2 = mem_k_p.reshape(depth * n_b * mem_len, d)
    mvp2 = mem_v_p.reshape(depth * n_b * mem_len, d)
    x, xb = _layer_tail(x, xb, 0, mkp2, mvp2, n_b, seq, BF16, wts, alpha, "p0")

    pc = _proj(xb, w_main_c, BF16, gla_modes, name="in_c_p")
    la = _gla_gate(xb, w_z, w_gate_b, b_gate_c, name="gla_gate_p")
    gla0 = jnp.zeros((n_b, gh, dk, dv), F32)
    gla_chunk = 128 if seq % 128 == 0 else seq
    gla_o, gla_p = _gla(pc, [0, qk_w // dk, 2 * qk_w // dv, (2 * qk_w + d) // dv], la, gla0, n_b, seq, gla_chunk,
                        BF16, name="gla_p")
    x, xb = _proj_ln([gla_o], [w_out_c_b], x, ln_g[1, 0], ln_b[1, 0], alpha, name="out_c_p")
    y_p, _ = _layer_tail(x, xb, 1, mkp2, mvp2, n_b, seq, BF16, wts, alpha, "p1")

    cos_s, sin_s = _rope_tables(past + jnp.arange(t_n, dtype=jnp.int32))
    rope_s = (jnp.tile(cos_s, (n_bs, 1)), jnp.tile(sin_s, (n_bs, 1)))
    xs = x_sample.reshape(n_s, d)
    s_modes = a_modes + [(True, 1.0), (False, 1.0)]
    sa = _proj(xs, w_in_a_b, F32, s_modes, rope=rope_s, name="in_a_s")
    ret_os, ret_s = _retention(sa, [0, rw // hb, 2 * rw // hb, 3 * rw // hb], state_ret, n_bs, t_n, t_n, F32,
                               name="retention_s")
    mq_s = sa[:, 4 * rw:4 * rw + mw]
    mk_s = sa[:, 4 * rw + mw:4 * rw + 2 * mw]
    mv_s = sa[:, 4 * rw + 2 * mw:]
    moba_os = _moba_sample(mq_s, mk_s, mv_s, cache_moba_k, cache_moba_v, page_table)
    x, xb = _proj_ln([ret_os, moba_os], [w_out_a_b[:rw], w_out_a_b[rw:]], xs, ln_g[0, 0], ln_b[0, 0], alpha,
                     name="out_a_s")
    cmk = cache_mem_k.reshape(depth * n_bs * mem_len, d)
    cmv = cache_mem_v.reshape(depth * n_bs * mem_len, d)
    x, xb = _layer_tail(x, xb, 0, cmk, cmv, n_bs, t_n, F32, wts, alpha, "s0")
    sc = _proj(xb, w_main_c, F32, gla_modes, name="in_c_s")
    la_s = _gla_gate(xb, w_z, w_gate_b, b_gate_c, name="gla_gate_s")
    gla_os, gla_s = _gla(sc, [0, qk_w // dk, 2 * qk_w // dv, (2 * qk_w + d) // dv], la_s, state_gla, n_bs, t_n, t_n,
                         F32, name="gla_s")
    x, xb = _proj_ln([gla_os], [w_out_c_b], x, ln_g[1, 0], ln_b[1, 0], alpha, name="out_c_s")
    y_s, _ = _layer_tail(x, xb, 1, cmk, cmv, n_bs, t_n, F32, wts, alpha, "s1")

    return (y_p.reshape(n_b, seq, d), y_s.reshape(n_bs, t_n, d), ret_p,
            mk_p.reshape(n_b, seq, MOBA_HEADS, HEAD_DIM), mv_p.reshape(n_b, seq, MOBA_HEADS, HEAD_DIM), gla_p,
            mem_k_p.reshape(depth, n_b, mem_len, MEM_HEADS, d // MEM_HEADS),
            mem_v_p.reshape(depth, n_b, mem_len, MEM_HEADS, d // MEM_HEADS),
            ret_s, mk_s.reshape(n_bs, t_n, MOBA_HEADS, HEAD_DIM), mv_s.reshape(n_bs, t_n, MOBA_HEADS, HEAD_DIM),
            gla_s)
```

```python
import functools

import jax
import jax.numpy as jnp
import numpy as np
from jax import lax
from jax.experimental import pallas as pl
from jax.experimental.pallas import tpu as pltpu

F32 = jnp.float32
BF16 = jnp.bfloat16

HEAD_DIM = 128
RET_HEADS = 8
MOBA_HEADS = 8
MOBA_BLOCK = 256
MOBA_TOPK = 3
GLA_HEADS = 4
GLA_GATE_RANK = 16
GLA_TAU = 16.0
MEM_HEADS = 4
PAGE_SIZE = 128
ROPE_THETA = 10000.0
LN_EPS = 1e-5

LANES = 128
SUBLANES = 8
V7X_VMEM_BYTES = 64 * 2**20
VMEM_LIMIT = 52 * 2**20

NEG = -1e30
EXP_CLAMP = 80.0


def _cp(*sem):
    return pltpu.CompilerParams(dimension_semantics=sem, vmem_limit_bytes=VMEM_LIMIT)


def _dot(a, b):
    return jnp.dot(a, b, preferred_element_type=F32)


def _dot_nt(a, b):
    return lax.dot_general(a, b, (((1,), (1,)), ((), ())), preferred_element_type=F32)


def _dot_tn(a_f32, b):
    return jnp.dot(a_f32.T.astype(BF16), b, preferred_element_type=F32)


def _row_tile(m, pref):
    t = min(m, pref)
    while m % t:
        t -= 8
    assert t > 0 and t % 8 == 0
    return t


def _silu(x):
    return x * jax.nn.sigmoid(x)


def _head_norm(o):
    mu = jnp.mean(o, axis=-1, keepdims=True)
    d = o - mu
    var = jnp.mean(d * d, axis=-1, keepdims=True)
    return d * lax.rsqrt(var + LN_EPS)


def _ln_store(y, g_ref, b_ref, o_ref, ob_ref):
    mu = jnp.mean(y, axis=-1, keepdims=True)
    d = y - mu
    var = jnp.mean(d * d, axis=-1, keepdims=True)
    out = d * lax.rsqrt(var + LN_EPS) * g_ref[...] + b_ref[...]
    o_ref[...] = out
    ob_ref[...] = out.astype(BF16)


def _any_of(j, values):
    pred = None
    for v in values:
        p = j == v
        pred = p if pred is None else jnp.logical_or(pred, p)
    return pred


def _proj_kernel(*refs, modes, groups, tn):
    use_rope = any(m[0] for m in modes)
    n_in = 4 if use_rope else 2
    x_ref, w_ref = refs[:2]
    cos_ref, sin_ref = refs[2:4] if use_rope else (None, None)
    o_refs = refs[n_in:n_in + len(groups)]
    acc = _dot(x_ref[...].astype(BF16), w_ref[...])
    j = pl.program_id(2)

    def store(mode, o_ref):
        do_rope, scale = mode
        if not do_rope:
            v = acc if scale == 1.0 else acc * scale
            o_ref[...] = v.astype(o_ref.dtype)
            return
        c = cos_ref[...]
        s = sin_ref[...]
        for h in range(tn // HEAD_DIM):
            sl = slice(h * HEAD_DIM, (h + 1) * HEAD_DIM)
            a = acc[:, sl]
            r = a * c + pltpu.roll(a, HEAD_DIM // 2, 1) * s
            if scale != 1.0:
                r = r * scale
            o_ref[:, sl] = r.astype(o_ref.dtype)

    cases = {}
    for jj, m in enumerate(modes):
        k = next(i for i, (s0, n) in enumerate(groups) if s0 <= jj < s0 + n)
        cases.setdefault((m, k), []).append(jj)
    if len(cases) == 1:
        (m, k), = cases
        store(m, o_refs[k])
        return
    for (m, k), js in sorted(cases.items()):
        pl.when(_any_of(j, js))(functools.partial(store, m, o_refs[k]))


def _proj(x, w, outs, modes=None, tn=1024, rope=None, col0=0, layer=0, name="proj"):
    if w.ndim == 2:
        w = w.reshape(1, *w.shape)
    m, k = x.shape
    nt = sum(n for _, n in outs)
    assert col0 % tn == 0 and col0 + nt * tn <= w.shape[2]
    modes = tuple(modes) if modes is not None else ((False, 1.0),) * nt
    assert len(modes) == nt
    tm = _row_tile(m, 1024)
    t0 = col0 // tn
    n_layers = w.shape[0] if layer is None else 1
    lsel = (lambda g: g) if layer is None else (lambda g: layer)
    in_specs = [pl.BlockSpec((tm, k), lambda g, i, j: (i, 0)),
                pl.BlockSpec((None, k, tn), lambda g, i, j: (lsel(g), 0, t0 + j))]
    args = [x, w]
    if any(mm[0] for mm in modes):
        cos, sin = rope
        p_rows = cos.shape[0]
        assert p_rows % tm == 0
        nblk = p_rows // tm
        in_specs += [pl.BlockSpec((tm, HEAD_DIM), lambda g, i, j: (i % nblk, 0))] * 2
        args += [cos, sin]
    groups, out_shapes, out_specs, start = [], [], [], 0
    for dtype, n in outs:
        groups.append((start, n))
        out_shapes.append(jax.ShapeDtypeStruct((n_layers, m, n * tn), dtype))
        out_specs.append(pl.BlockSpec((None, tm, tn),
                                      lambda g, i, j, s=start, n=n: (g, i, jnp.clip(j - s, 0, n - 1))))
        start += n
    res = pl.pallas_call(
        functools.partial(_proj_kernel, modes=modes, groups=tuple(groups), tn=tn),
        out_shape=tuple(out_shapes),
        grid=(n_layers, m // tm, nt),
        in_specs=in_specs,
        out_specs=tuple(out_specs),
        compiler_params=_cp("parallel", "parallel", "arbitrary"),
        name=name,
    )(*args)
    return res if layer is None else tuple(r.reshape(r.shape[1:]) for r in res)


def _proj_ln_kernel(*refs, n_in, alpha):
    a_refs = refs[:n_in]
    w_refs = refs[n_in:2 * n_in]
    x_ref, g_ref, b_ref, o_ref, ob_ref = refs[2 * n_in:]
    acc = None
    for a_ref, w_ref in zip(a_refs, w_refs):
        d = _dot(a_ref[...].astype(BF16), w_ref[...])
        acc = d if acc is None else acc + d
    _ln_store(alpha * x_ref[...] + acc, g_ref, b_ref, o_ref, ob_ref)


def _proj_ln(a_list, w, x, g, b, alpha, layer=0, name="proj_ln"):
    if w.ndim == 2:
        w = w.reshape(1, *w.shape)
    m, d = x.shape
    tm = _row_tile(m, 512)
    in_specs = [pl.BlockSpec((tm, a.shape[1]), lambda i: (i, 0)) for a in a_list]
    row0 = 0
    for a in a_list:
        rows = a.shape[1]
        assert row0 % rows == 0
        in_specs.append(pl.BlockSpec((None, rows, d), lambda i, r=row0 // rows: (layer, r, 0),
                                     pipeline_mode=pl.Buffered(1)))
        row0 += rows
    assert row0 == w.shape[1]
    in_specs += [pl.BlockSpec((tm, d), lambda i: (i, 0)),
                 pl.BlockSpec((1, d), lambda i: (0, 0)),
                 pl.BlockSpec((1, d), lambda i: (0, 0))]
    return pl.pallas_call(
        functools.partial(_proj_ln_kernel, n_in=len(a_list), alpha=alpha),
        out_shape=(jax.ShapeDtypeStruct((m, d), F32), jax.ShapeDtypeStruct((m, d), BF16)),
        grid=(m // tm,),
        in_specs=in_specs,
        out_specs=(pl.BlockSpec((tm, d), lambda i: (i, 0)), pl.BlockSpec((tm, d), lambda i: (i, 0))),
        compiler_params=_cp("parallel"),
        name=name,
    )(*a_list, *([w] * len(a_list)), x, g.reshape(1, d), b.reshape(1, d))


def _ffn_kernel(xb_ref, x_ref, wg_ref, wu_ref, wd_ref, g_ref, b_ref, o_ref, ob_ref, acc_ref, *, alpha):
    f = pl.program_id(1)

    @pl.when(f == 0)
    def _():
        acc_ref[...] = jnp.zeros_like(acc_ref)

    xb = xb_ref[...]
    gate = _dot(xb, wg_ref[...])
    up = _dot(xb, wu_ref[...])
    hid = (_silu(gate) * up).astype(BF16)
    acc_ref[...] += _dot(hid, wd_ref[...])

    @pl.when(f == pl.num_programs(1) - 1)
    def _():
        _ln_store(alpha * x_ref[...] + acc_ref[...], g_ref, b_ref, o_ref, ob_ref)


def _ffn(xb, x, w_gu, w_down, layer, g, b, alpha, name="ffn"):
    m, d = x.shape
    ff = w_down.shape[1]
    tm = _row_tile(m, 512)
    tf = 512 if ff % 512 == 0 else 256
    assert ff % tf == 0
    nf = ff // tf
    return pl.pallas_call(
        functools.partial(_ffn_kernel, alpha=alpha),
        out_shape=(jax.ShapeDtypeStruct((m, d), F32), jax.ShapeDtypeStruct((m, d), BF16)),
        grid=(m // tm, nf),
        in_specs=[pl.BlockSpec((tm, d), lambda i, f: (i, 0)),
                  pl.BlockSpec((tm, d), lambda i, f: (i, 0)),
                  pl.BlockSpec((None, d, tf), lambda i, f: (layer, 0, f)),
                  pl.BlockSpec((None, d, tf), lambda i, f: (layer, 0, nf + f)),
                  pl.BlockSpec((None, tf, d), lambda i, f: (layer, f, 0)),
                  pl.BlockSpec((1, d), lambda i, f: (0, 0)),
                  pl.BlockSpec((1, d), lambda i, f: (0, 0))],
        out_specs=(pl.BlockSpec((tm, d), lambda i, f: (i, 0)), pl.BlockSpec((tm, d), lambda i, f: (i, 0))),
        scratch_shapes=[pltpu.VMEM((tm, d), F32)],
        compiler_params=_cp("parallel", "arbitrary"),
        name=name,
    )(xb, x, w_gu, w_gu, w_down, g.reshape(1, d), b.reshape(1, d))


def _ret_kernel(q_ref, k_ref, v_ref, g_ref, s0_ref, dmat_ref, qdec_ref, kdec_ref, gc_ref,
                o_ref, sout_ref, s_scr):
    c = pl.program_id(1)

    @pl.when(c == 0)
    def _():
        s_scr[...] = s0_ref[0]

    for h in range(RET_HEADS):
        sl = slice(h * HEAD_DIM, (h + 1) * HEAD_DIM)
        q = q_ref[:, sl].astype(F32)
        k = k_ref[:, sl].astype(F32)
        v = v_ref[:, sl].astype(BF16)
        s_prev = s_scr[h]
        att = _dot_nt(q.astype(BF16), k.astype(BF16)) * dmat_ref[h]
        o = _dot(att.astype(BF16), v) + _dot((q * qdec_ref[h]).astype(BF16), s_prev.astype(BF16))
        s_scr[h] = gc_ref[h] * s_prev + _dot_tn(k * kdec_ref[h], v)
        o_ref[:, sl] = (_head_norm(o) * _silu(g_ref[:, sl].astype(F32))).astype(o_ref.dtype)

    @pl.when(c == pl.num_programs(1) - 1)
    def _():
        sout_ref[0] = s_scr[...]


def _retention(arr, s0, n_seq, seq_len, chunk, out_dtype, name="retention"):
    nc = seq_len // chunk
    h_n = RET_HEADS
    width = h_n * HEAD_DIM
    lg = np.log1p(-np.power(2.0, -5.0 - np.arange(h_n, dtype=np.float64)))[:, None, None]
    i = np.arange(chunk, dtype=np.float64)
    diff = i[:, None] - i[None, :]
    dmat = np.where(diff >= 0, np.exp(lg * np.maximum(diff, 0.0)), 0.0).astype(np.float32)
    qdec = np.broadcast_to(np.exp(lg * (i[None, :, None] + 1.0)), (h_n, chunk, HEAD_DIM)).astype(np.float32)
    kdec = np.broadcast_to(np.exp(lg * (chunk - 1.0 - i[None, :, None])), (h_n, chunk, HEAD_DIM)).astype(np.float32)
    gc = np.broadcast_to(np.exp(lg * chunk), (h_n, 1, HEAD_DIM)).astype(np.float32)

    def sec_spec(sec):
        return pl.BlockSpec((chunk, width), lambda b, c: (b * nc + c, sec))

    def table_spec(shape):
        return pl.BlockSpec(shape, lambda b, c: (0, 0, 0))

    state_spec = pl.BlockSpec((1, h_n, HEAD_DIM, HEAD_DIM), lambda b, c: (b, 0, 0, 0))
    return pl.pallas_call(
        _ret_kernel,
        out_shape=(jax.ShapeDtypeStruct((n_seq * seq_len, width), out_dtype),
                   jax.ShapeDtypeStruct(s0.shape, F32)),
        grid=(n_seq, nc),
        in_specs=[sec_spec(0), sec_spec(1), sec_spec(2), sec_spec(3), state_spec,
                  table_spec(dmat.shape), table_spec(qdec.shape), table_spec(kdec.shape), table_spec(gc.shape)],
        out_specs=(sec_spec(0), state_spec),
        scratch_shapes=[pltpu.VMEM((h_n, HEAD_DIM, HEAD_DIM), F32)],
        compiler_params=_cp("parallel", "arbitrary"),
        name=name,
    )(arr, arr, arr, arr, s0, jnp.asarray(dmat), jnp.asarray(qdec), jnp.asarray(kdec), jnp.asarray(gc))


def _moba_prompt_kernel(q_ref, k_ref, v_ref, o_ref, kmean_scr, kb_scr, vb_scr, *, nb, n_sel):
    qi = pl.program_id(2)
    blk_rows = MOBA_BLOCK
    scale = HEAD_DIM ** -0.5

    @pl.when(qi == 0)
    def _():
        for n in range(nb):
            rows = slice(n * blk_rows, (n + 1) * blk_rows)
            kmean_scr[n:n + 1, :] = jnp.mean(k_ref[rows, :], axis=0, keepdims=True)
            kb_scr[rows, :] = k_ref[rows, :].astype(BF16)
            vb_scr[rows, :] = v_ref[rows, :].astype(BF16)

    q = q_ref[...]
    gate = _dot_nt(q, kmean_scr[...].astype(BF16))
    blk = lax.broadcasted_iota(jnp.int32, gate.shape, 1)
    valid = blk < qi
    gate = jnp.where(valid, gate, -jnp.inf)
    rank = jnp.zeros(gate.shape, jnp.int32)
    for m in range(nb):
        gm = gate[:, m:m + 1]
        beats = jnp.where(gm > gate, 1, jnp.where(gm == gate, jnp.where(blk > m, 1, 0), 0))
        rank = rank + beats
    selb = jnp.where(valid, jnp.where(rank < n_sel, 0.0, NEG), NEG)
    qs = (q.astype(F32) * scale).astype(BF16)
    r_i = lax.broadcasted_iota(jnp.int32, (blk_rows, blk_rows), 0)
    c_i = lax.broadcasted_iota(jnp.int32, (blk_rows, blk_rows), 1)

    def attend(n_past):
        n_keys = (n_past + 1) * blk_rows
        s = _dot_nt(qs, kb_scr[0:n_keys, :])
        parts = [s[:, n * blk_rows:(n + 1) * blk_rows] + selb[:, n:n + 1] for n in range(n_past)]
        parts.append(jnp.where(c_i <= r_i, s[:, n_past * blk_rows:], NEG))
        s = jnp.concatenate(parts, axis=1) if n_past else parts[0]
        m = jnp.max(s, axis=-1, keepdims=True)
        p = jnp.exp(s - m)
        l = jnp.sum(p, axis=-1, keepdims=True)
        o_ref[...] = (_dot(p.astype(BF16), vb_scr[0:n_keys, :]) / l).astype(o_ref.dtype)

    for n_past in range(nb):
        pl.when(qi == n_past)(functools.partial(attend, n_past))


def _moba_prompt(q_arr, q_col, k_arr, v_arr, n_seq, seq_len, name="moba_prompt"):
    assert seq_len % MOBA_BLOCK == 0
    nb = seq_len // MOBA_BLOCK
    n_sel = min(MOBA_TOPK, nb - 1)
    return pl.pallas_call(
        functools.partial(_moba_prompt_kernel, nb=nb, n_sel=n_sel),
        out_shape=jax.ShapeDtypeStruct((n_seq * seq_len, MOBA_HEADS * HEAD_DIM), BF16),
        grid=(n_seq, MOBA_HEADS, nb),
        in_specs=[pl.BlockSpec((MOBA_BLOCK, HEAD_DIM), lambda b, h, i: (b * nb + i, q_col + h)),
                  pl.BlockSpec((seq_len, HEAD_DIM), lambda b, h, i: (b, h)),
                  pl.BlockSpec((seq_len, HEAD_DIM), lambda b, h, i: (b, h))],
        out_specs=pl.BlockSpec((MOBA_BLOCK, HEAD_DIM), lambda b, h, i: (b * nb + i, h)),
        scratch_shapes=[pltpu.VMEM((nb, HEAD_DIM), F32),
                        pltpu.VMEM((seq_len, HEAD_DIM), BF16),
                        pltpu.VMEM((seq_len, HEAD_DIM), BF16)],
        compiler_params=_cp("parallel", "parallel", "arbitrary"),
        name=name,
    )(q_arr, k_arr, v_arr)


def _moba_sample_kernel(pt_ref, q_ref, kn_ref, vn_ref, bias_ref, own_bias_ref, *refs, n_pg, n_blk, n_sel):
    k_pages = refs[:n_pg]
    v_pages = refs[n_pg:2 * n_pg]
    o_ref, m_scr, l_scr, acc_scr, km_scr = refs[2 * n_pg:]
    c = pl.program_id(1)
    rows = q_ref.shape[0]
    scale = HEAD_DIM ** -0.5
    ppb = MOBA_BLOCK // PAGE_SIZE
    page_rows = PAGE_SIZE * MOBA_HEADS
    qf = q_ref[...]
    qs = (qf * scale).astype(BF16)
    bias = bias_ref[...]

    for j in range(n_pg // ppb):
        pgs = range(j * ppb, (j + 1) * ppb)
        s = jnp.concatenate([_dot_nt(qs, k_pages[p][...].astype(BF16)) for p in pgs], axis=1) + bias
        m = jnp.max(s, axis=-1, keepdims=True)
        p_ = jnp.exp(s - m)
        l = jnp.sum(p_, axis=-1, keepdims=True)
        acc = None
        for i, p in enumerate(pgs):
            d = _dot(p_[:, i * page_rows:(i + 1) * page_rows].astype(BF16), v_pages[p][...].astype(BF16))
            acc = d if acc is None else acc + d
        ksum = None
        for p in pgs:
            d = jnp.sum(k_pages[p][...].reshape(PAGE_SIZE, MOBA_HEADS, HEAD_DIM), axis=0)
            ksum = d if ksum is None else ksum + d
        n_glob = c * (n_pg // ppb) + j
        acc_scr[n_glob] = acc
        m_scr[n_glob] = jnp.broadcast_to(m, (rows, HEAD_DIM))
        l_scr[n_glob] = jnp.broadcast_to(l, (rows, HEAD_DIM))
        km_scr[n_glob] = ksum * (1.0 / MOBA_BLOCK)

    @pl.when(c == pl.num_programs(1) - 1)
    def _():
        reps = rows // MOBA_HEADS
        gates = [jnp.sum(qf * jnp.concatenate([km_scr[n]] * reps, axis=0), axis=-1, keepdims=True)
                 for n in range(n_blk)]
        s_own = _dot_nt(qs, kn_ref[...].astype(BF16)) + own_bias_ref[...]
        m_tot = jnp.max(s_own, axis=-1, keepdims=True)
        sel = []
        for n in range(n_blk):
            rank = jnp.zeros((rows, 1), jnp.int32)
            for mm in range(n_blk):
                if mm == n:
                    continue
                beats = (gates[mm] >= gates[n]) if mm < n else (gates[mm] > gates[n])
                rank = rank + jnp.where(beats, 1, 0)
            sel.append(rank < n_sel)
            m_tot = jnp.maximum(m_tot, jnp.where(sel[n], m_scr[n][:, :1], NEG))
        p_own = jnp.exp(s_own - m_tot)
        den = jnp.sum(p_own, axis=-1, keepdims=True)
        num = _dot(p_own.astype(BF16), vn_ref[...].astype(BF16))
        for n in range(n_blk):
            w = jnp.where(sel[n], jnp.exp(m_scr[n] - m_tot), 0.0)
            num = num + w * acc_scr[n]
            den = den + w[:, :1] * l_scr[n][:, :1]
        o_ref[...] = num / den


def _moba_sample(q, kn, vn, cache_k, cache_v, page_table, name="moba_sample"):
    bs, n_pages = page_table.shape
    rows = q.shape[0] // bs
    t_n = rows // MOBA_HEADS
    past = n_pages * PAGE_SIZE
    assert past % MOBA_BLOCK == 0 and t_n <= MOBA_BLOCK and rows % SUBLANES == 0
    n_blk = past // MOBA_BLOCK
    n_sel = min(MOBA_TOPK, n_blk)
    ppb = MOBA_BLOCK // PAGE_SIZE
    n_pg = 8 if n_pages % 8 == 0 else ppb
    n_steps = n_pages // n_pg
    page_rows = PAGE_SIZE * MOBA_HEADS
    ck = cache_k.reshape(cache_k.shape[0], page_rows, HEAD_DIM)
    cv = cache_v.reshape(cache_v.shape[0], page_rows, HEAD_DIM)
    r_head = np.arange(rows) % MOBA_HEADS
    c_head = np.arange(ppb * page_rows) % MOBA_HEADS
    bias = np.where(r_head[:, None] == c_head[None, :], 0.0, NEG).astype(np.float32)
    r_tok = np.arange(rows) // MOBA_HEADS
    own_ok = (r_head[:, None] == r_head[None, :]) & (r_tok[None, :] <= r_tok[:, None])
    own_bias = np.where(own_ok, 0.0, NEG).astype(np.float32)

    def page_spec(p):
        return pl.BlockSpec((None, page_rows, HEAD_DIM),
                            lambda b, c, pt: (pt[b * n_pages + c * n_pg + p], 0, 0))

    tok_spec = pl.BlockSpec((rows, HEAD_DIM), lambda b, c, pt: (b, 0))
    part = pltpu.VMEM((n_blk, rows, HEAD_DIM), F32)
    grid_spec = pltpu.PrefetchScalarGridSpec(
        num_scalar_prefetch=1,
        grid=(bs, n_steps),
        in_specs=[tok_spec, tok_spec, tok_spec,
                  pl.BlockSpec(bias.shape, lambda b, c, pt: (0, 0)),
                  pl.BlockSpec(own_bias.shape, lambda b, c, pt: (0, 0))] + [page_spec(p) for p in range(n_pg)] * 2,
        out_specs=tok_spec,
        scratch_shapes=[part, part, part, pltpu.VMEM((n_blk, MOBA_HEADS, HEAD_DIM), F32)],
    )
    return pl.pallas_call(
        functools.partial(_moba_sample_kernel, n_pg=n_pg, n_blk=n_blk, n_sel=n_sel),
        out_shape=jax.ShapeDtypeStruct(q.shape, F32),
        grid_spec=grid_spec,
        compiler_params=_cp("parallel", "arbitrary"),
        name=name,
    )(page_table.reshape(-1), q, kn, vn, jnp.asarray(bias), jnp.asarray(own_bias), *([ck] * n_pg), *([cv] * n_pg))


def _memattn_kernel(q_ref, k_ref, v_ref, o_ref, *, n_grp, q_rows, mem_len):
    d = q_ref.shape[1]
    hd = d // MEM_HEADS
    scale = hd ** -0.5
    for s_i in range(n_grp):
        qr = slice(s_i * q_rows, (s_i + 1) * q_rows)
        mr = slice(s_i * mem_len, (s_i + 1) * mem_len)
        for h in range(MEM_HEADS):
            sl = slice(h * hd, (h + 1) * hd)
            qh = (q_ref[qr, sl].astype(F32) * scale).astype(BF16)
            kh = k_ref[mr, sl].astype(BF16)
            vh = v_ref[mr, sl].astype(BF16)
            s = _dot_nt(qh, kh)
            m = jnp.max(s, axis=-1, keepdims=True)
            p = jnp.exp(s - m)
            l = jnp.sum(p, axis=-1, keepdims=True)
            o_ref[qr, sl] = (_dot(p.astype(BF16), vh) / l).astype(o_ref.dtype)


def _memattn_short_kernel(q_ref, k_ref, v_ref, o_ref, *, n_grp, q_rows, mem_len):
    d = q_ref.shape[1]
    hd = d // MEM_HEADS
    scale = hd ** -0.5
    lane_head = lax.broadcasted_iota(jnp.int32, (q_rows, d), 1) // hd
    for s_i in range(n_grp):
        qr = slice(s_i * q_rows, (s_i + 1) * q_rows)
        mr = slice(s_i * mem_len, (s_i + 1) * mem_len)
        q = q_ref[qr, :].astype(F32) * scale
        q_heads = jnp.concatenate([jnp.where(lane_head == h, q, 0.0) for h in range(MEM_HEADS)], axis=0)
        s = _dot_nt(q_heads.astype(BF16), k_ref[mr, :].astype(BF16))
        m = jnp.max(s, axis=-1, keepdims=True)
        p = jnp.exp(s - m)
        inv = 1.0 / jnp.sum(p, axis=-1, keepdims=True)
        o_all = _dot(p.astype(BF16), v_ref[mr, :].astype(BF16)) * inv
        for h in range(MEM_HEADS):
            o_ref[qr, h * hd:(h + 1) * hd] = o_all[h * q_rows:(h + 1) * q_rows, h * hd:(h + 1) * hd].astype(
                o_ref.dtype)


def _memattn(q, mem_k, mem_v, layer, n_seq, seq_len, mem_len, out_dtype, name="memattn"):
    d = q.shape[1]
    tm = _row_tile(seq_len, 512)
    nt = seq_len // tm
    short = nt == 1 and tm * MEM_HEADS <= LANES
    n_grp = 2 if (short and n_seq % 2 == 0) else 1
    n_b = n_seq // n_grp
    kv_spec = pl.BlockSpec((n_grp * mem_len, d), lambda b, i: (layer * n_b + b, 0))
    q_spec = pl.BlockSpec((n_grp * tm, d), lambda b, i: (b * nt + i, 0))
    body = _memattn_short_kernel if short else _memattn_kernel
    return pl.pallas_call(
        functools.partial(body, n_grp=n_grp, q_rows=tm, mem_len=mem_len),
        out_shape=jax.ShapeDtypeStruct(q.shape, out_dtype),
        grid=(n_b, nt),
        in_specs=[q_spec, kv_spec, kv_spec],
        out_specs=q_spec,
        compiler_params=_cp("parallel", "arbitrary"),
        name=name,
    )(q, mem_k, mem_v)


def _gla_gate_kernel(x_ref, wz_ref, wg_ref, b_ref, o_ref):
    z = _dot(x_ref[...].astype(BF16), wz_ref[...])
    g = _dot(z.astype(BF16), wg_ref[...]) + b_ref[...]
    o_ref[...] = (jnp.minimum(g, 0.0) - jnp.log1p(jnp.exp(-jnp.abs(g)))) * (1.0 / GLA_TAU)


def _gla_gate(x, w_z, w_gate, b_gate, name="gla_gate"):
    m, d = x.shape
    n = w_gate.shape[1]
    tm = _row_tile(m, 512)
    return pl.pallas_call(
        _gla_gate_kernel,
        out_shape=jax.ShapeDtypeStruct((m, n), F32),
        grid=(m // tm,),
        in_specs=[pl.BlockSpec((tm, d), lambda i: (i, 0)),
                  pl.BlockSpec(w_z.shape, lambda i: (0, 0)),
                  pl.BlockSpec(w_gate.shape, lambda i: (0, 0)),
                  pl.BlockSpec((1, n), lambda i: (0, 0))],
        out_specs=pl.BlockSpec((tm, n), lambda i: (i, 0)),
        compiler_params=_cp("parallel"),
        name=name,
    )(x, w_z, w_gate, b_gate.reshape(1, n))


def _gla_kernel(q_ref, k_ref, v_ref, r_ref, la_ref, s0_ref, o_ref, sout_ref, s_scr):
    c = pl.program_id(1)
    chunk = q_ref.shape[0]
    h_n, dk, dv = s_scr.shape

    @pl.when(c == 0)
    def _():
        s_scr[...] = s0_ref[0]

    r_i = lax.broadcasted_iota(jnp.int32, (chunk, chunk), 0)
    c_i = lax.broadcasted_iota(jnp.int32, (chunk, chunk), 1)
    causal = c_i <= r_i
    tri = jnp.where(causal, 1.0, 0.0).astype(BF16)
    for h in range(h_n):
        ksl = slice(h * dk, (h + 1) * dk)
        vsl = slice(h * dv, (h + 1) * dv)
        q = q_ref[:, ksl].astype(F32)
        k = k_ref[:, ksl].astype(F32)
        v = v_ref[:, vsl].astype(BF16)
        la = la_ref[:, ksl]
        la_hi = la.astype(BF16)
        rem = la - la_hi.astype(F32)
        la_mid = rem.astype(BF16)
        la_lo = (rem - la_mid.astype(F32)).astype(BF16)
        b = _dot(tri, la_hi) + _dot(tri, la_mid) + _dot(tri, la_lo)
        b_last = b[chunk - 1:chunk, :]
        b_mid = b[chunk // 2 - 1:chunk // 2, :]
        qt = (q * jnp.exp(jnp.minimum(b - b_mid, EXP_CLAMP))).astype(BF16)
        kt = (k * jnp.exp(jnp.minimum(b_mid - b, EXP_CLAMP))).astype(BF16)
        att = jnp.where(causal, _dot_nt(qt, kt), 0.0)
        s_prev = s_scr[h]
        o = _dot(att.astype(BF16), v) + _dot((q * jnp.exp(b)).astype(BF16), s_prev.astype(BF16))
        upd = _dot_tn(k * jnp.exp(b_last - b), v)
        col = jnp.transpose(jnp.broadcast_to(jnp.exp(b_last), (LANES, dk)))
        for jv in range(dv // LANES):
            sl = slice(jv * LANES, (jv + 1) * LANES)
            s_scr[h, :, sl] = col * s_prev[:, sl] + upd[:, sl]
        o_ref[:, vsl] = (_head_norm(o) * _silu(r_ref[:, vsl].astype(F32))).astype(o_ref.dtype)

    @pl.when(c == pl.num_programs(1) - 1)
    def _():
        sout_ref[0] = s_scr[...]


def _gla(arr, la, s0, n_seq, seq_len, chunk, out_dtype, name="gla"):
    _, h_n, dk, dv = s0.shape
    nc = seq_len // chunk
    qk_w, v_w = h_n * dk, h_n * dv
    assert (2 * qk_w) % v_w == 0
    state_spec = pl.BlockSpec((1, h_n, dk, dv), lambda b, c: (b, 0, 0, 0))

    def sec_spec(width, blk):
        return pl.BlockSpec((chunk, width), lambda b, c: (b * nc + c, blk))

    return pl.pallas_call(
        _gla_kernel,
        out_shape=(jax.ShapeDtypeStruct((n_seq * seq_len, v_w), out_dtype),
                   jax.ShapeDtypeStruct(s0.shape, F32)),
        grid=(n_seq, nc),
        in_specs=[sec_spec(qk_w, 0), sec_spec(qk_w, 1), sec_spec(v_w, 2 * qk_w // v_w),
                  sec_spec(v_w, 2 * qk_w // v_w + 1), sec_spec(qk_w, 0), state_spec],
        out_specs=(sec_spec(v_w, 0), state_spec),
        scratch_shapes=[pltpu.VMEM((h_n, dk, dv), F32)],
        compiler_params=_cp("parallel", "arbitrary"),
        name=name,
    )(arr, arr, arr, arr, la, s0)


def _rope_tables(pos):
    half = HEAD_DIM // 2
    inv_freq = jnp.power(ROPE_THETA, -jnp.arange(half, dtype=F32) / half)
    ang = pos.astype(F32)[:, None] * inv_freq[None, :]
    cos, sin = jnp.cos(ang), jnp.sin(ang)
    return jnp.concatenate([cos, cos], axis=-1), jnp.concatenate([-sin, sin], axis=-1)


def _layer_tail(x, xb, layer, mem_k, mem_v, n_seq, seq_len, per_seq_dtype, wts, alpha, tag):
    ln_g, ln_b = wts["ln_g"], wts["ln_b"]
    d = x.shape[1]
    mem_len = mem_k.shape[0] // (wts["depth"] * n_seq)
    q, = _proj(xb, wts["w_mem_q"], [(per_seq_dtype, d // 1024)], layer=layer, name=f"memq_{tag}")
    o = _memattn(q, mem_k, mem_v, layer, n_seq, seq_len, mem_len, per_seq_dtype, name=f"memattn_{tag}")
    x, xb = _proj_ln([o], wts["w_mem_o"], x, ln_g[layer, 1], ln_b[layer, 1], alpha, layer=layer,
                     name=f"memo_{tag}")
    return _ffn(xb, x, wts["w_ffn_gu"], wts["w_ffn_down"], layer, ln_g[layer, 2], ln_b[layer, 2], alpha,
                name=f"ffn_{tag}")


def kernel(x_prompt, x_sample, mem_prompt, state_ret, cache_moba_k, cache_moba_v, page_table, state_gla,
           cache_mem_k, cache_mem_v, w_in_a, w_out_a, w_in_c, w_gate_c, b_gate_c, w_out_c,
           w_mem_q, w_mem_k, w_mem_v, w_mem_o, w_ffn_gu, w_ffn_down, ln_g, ln_b):
    n_b, seq, d = x_prompt.shape
    n_bs, t_n, _ = x_sample.shape
    mem_len = mem_prompt.shape[1]
    depth = w_mem_q.shape[0]
    alpha = float((2 * depth) ** 0.25)
    past = page_table.shape[1] * PAGE_SIZE
    n_p, n_s = n_b * seq, n_bs * t_n
    rw = RET_HEADS * HEAD_DIM
    mw = MOBA_HEADS * HEAD_DIM
    _, gh, dk, dv = state_gla.shape
    qk_w = gh * dk
    tn = 1024
    assert rw == tn and mw == tn and qk_w == tn and d % tn == 0
    k_scale = HEAD_DIM ** -0.5

    wts = {
        "depth": depth, "ln_g": ln_g, "ln_b": ln_b,
        "w_mem_q": w_mem_q.astype(BF16), "w_mem_o": w_mem_o.astype(BF16),
        "w_ffn_gu": w_ffn_gu.astype(BF16), "w_ffn_down": w_ffn_down.astype(BF16),
    }
    w_in_a_b = w_in_a.astype(BF16)
    w_out_a_b = w_out_a.astype(BF16)
    w_in_c_b = w_in_c.astype(BF16)
    w_out_c_b = w_out_c.astype(BF16)
    w_mem_k_b = w_mem_k.astype(BF16)
    w_mem_v_b = w_mem_v.astype(BF16)
    w_z = w_in_c_b[:, 2 * qk_w + 2 * d:]
    w_gate_b = w_gate_c.astype(BF16)
    n_c_tiles = (2 * qk_w + 2 * d) // tn
    gla_modes = [(False, 1.0)] * n_c_tiles
    gla_modes[1] = (False, float(dk ** -0.5))
    a_modes = [(True, 1.0), (True, k_scale), (False, 1.0), (False, 1.0), (True, 1.0), (True, 1.0), (False, 1.0)]

    rope_p = _rope_tables(jnp.arange(seq, dtype=jnp.int32))
    xp = x_prompt.reshape(n_p, d)
    xpb = xp.astype(BF16)
    mem_b = mem_prompt.reshape(n_b * mem_len, d).astype(BF16)
    mem_k_p, = _proj(mem_b, w_mem_k_b, [(F32, d // tn)], layer=None, name="memk_p")
    mem_v_p, = _proj(mem_b, w_mem_v_b, [(F32, d // tn)], layer=None, name="memv_p")

    pa, mk_p, mv_p = _proj(xpb, w_in_a_b, [(BF16, 5), (F32, 1), (F32, 1)], a_modes, rope=rope_p, name="in_a_p")
    ret0 = jnp.zeros((n_b, RET_HEADS, HEAD_DIM, HEAD_DIM), F32)
    ret_chunk = 256 if seq % 256 == 0 else seq
    ret_o, ret_p = _retention(pa, ret0, n_b, seq, ret_chunk, BF16, name="retention_p")
    moba_o = _moba_prompt(pa, 4 * rw // HEAD_DIM, mk_p, mv_p, n_b, seq)
    x, xb = _proj_ln([ret_o, moba_o], w_out_a_b, xp, ln_g[0, 0], ln_b[0, 0], alpha, name="out_a_p")
    mkp2 = mem_k_p.reshape(depth * n_b * mem_len, d)
    mvp2 = mem_v_p.reshape(depth * n_b * mem_len, d)
    x, xb = _layer_tail(x, xb, 0, mkp2, mvp2, n_b, seq, BF16, wts, alpha, "p0")

    pc, = _proj(xb, w_in_c_b, [(BF16, n_c_tiles)], gla_modes, name="in_c_p")
    la = _gla_gate(xb, w_z, w_gate_b, b_gate_c, name="gla_gate_p")
    gla0 = jnp.zeros((n_b, gh, dk, dv), F32)
    gla_chunk = 128 if seq % 128 == 0 else seq
    gla_o, gla_p = _gla(pc, la, gla0, n_b, seq, gla_chunk, BF16, name="gla_p")
    x, xb = _proj_ln([gla_o], w_out_c_b, x, ln_g[1, 0], ln_b[1, 0], alpha, name="out_c_p")
    y_p, _ = _layer_tail(x, xb, 1, mkp2, mvp2, n_b, seq, BF16, wts, alpha, "p1")

    cos_s, sin_s = _rope_tables(past + jnp.arange(t_n, dtype=jnp.int32))
    rope_s = (jnp.tile(cos_s, (n_bs, 1)), jnp.tile(sin_s, (n_bs, 1)))
    xs = x_sample.reshape(n_s, d)
    sa, = _proj(xs, w_in_a_b, [(F32, 7)], a_modes, rope=rope_s, name="in_a_s")
    ret_os, ret_s = _retention(sa, state_ret, n_bs, t_n, t_n, F32, name="retention_s")
    mq_s = sa[:, 4 * rw:4 * rw + mw]
    mk_s = sa[:, 4 * rw + mw:4 * rw + 2 * mw]
    mv_s = sa[:, 4 * rw + 2 * mw:]
    per_head = (n_s * MOBA_HEADS, HEAD_DIM)
    moba_os = _moba_sample(mq_s.reshape(per_head), mk_s.reshape(per_head), mv_s.reshape(per_head),
                           cache_moba_k, cache_moba_v, page_table).reshape(n_s, mw)
    x, xb = _proj_ln([ret_os, moba_os], w_out_a_b, xs, ln_g[0, 0], ln_b[0, 0], alpha, name="out_a_s")
    cmk = cache_mem_k.reshape(depth * n_bs * mem_len, d)
    cmv = cache_mem_v.reshape(depth * n_bs * mem_len, d)
    x, xb = _layer_tail(x, xb, 0, cmk, cmv, n_bs, t_n, F32, wts, alpha, "s0")
    sc, = _proj(xb, w_in_c_b, [(F32, n_c_tiles)], gla_modes, name="in_c_s")
    la_s = _gla_gate(xb, w_z, w_gate_b, b_gate_c, name="gla_gate_s")
    gla_os, gla_s = _gla(sc, la_s, state_gla, n_bs, t_n, t_n, F32, name="gla_s")
    x, xb = _proj_ln([gla_os], w_out_c_b, x, ln_g[1, 0], ln_b[1, 0], alpha, name="out_c_s")
    y_s, _ = _layer_tail(x, xb, 1, cmk, cmv, n_bs, t_n, F32, wts, alpha, "s1")

    return (y_p.reshape(n_b, seq, d), y_s.reshape(n_bs, t_n, d), ret_p,
            mk_p.reshape(n_b, seq, MOBA_HEADS, HEAD_DIM), mv_p.reshape(n_b, seq, MOBA_HEADS, HEAD_DIM), gla_p,
            mem_k_p.reshape(depth, n_b, mem_len, MEM_HEADS, d // MEM_HEADS),
            mem_v_p.reshape(depth, n_b, mem_len, MEM_HEADS, d // MEM_HEADS),
            ret_s, mk_s.reshape(n_bs, t_n, MOBA_HEADS, HEAD_DIM), mv_s.reshape(n_bs, t_n, MOBA_HEADS, HEAD_DIM),
            gla_s)
```

```python
import functools

import jax
import jax.numpy as jnp
import numpy as np
from jax import lax
from jax.experimental import pallas as pl
from jax.experimental.pallas import tpu as pltpu

F32 = jnp.float32
BF16 = jnp.bfloat16

HEAD_DIM = 128
RET_HEADS = 8
MOBA_HEADS = 8
MOBA_BLOCK = 256
MOBA_TOPK = 3
GLA_HEADS = 4
GLA_GATE_RANK = 16
GLA_TAU = 16.0
MEM_HEADS = 4
PAGE_SIZE = 128
ROPE_THETA = 10000.0
LN_EPS = 1e-5

LANES = 128
SUBLANES = 8
V7X_VMEM_BYTES = 64 * 2**20
VMEM_LIMIT = 52 * 2**20

PROJ_LN_SUB_ROWS = 128
PROJ_SUB_COLS = 256
MOBA_HEADS_PER_STEP = 4
CAST_BLOCK_BYTES = 4 * 2**20

NEG = -1e30
EXP_CLAMP = 80.0


def _cp(*sem):
    return pltpu.CompilerParams(dimension_semantics=sem, vmem_limit_bytes=VMEM_LIMIT)


def _dot(a, b):
    return jnp.dot(a, b, preferred_element_type=F32)


def _dot_nt(a, b):
    return lax.dot_general(a, b, (((1,), (1,)), ((), ())), preferred_element_type=F32)


def _dot_tn(a_f32, b):
    return jnp.dot(a_f32.T.astype(BF16), b, preferred_element_type=F32)


def _row_tile(m, pref):
    t = min(m, pref)
    while m % t:
        t -= 8
    assert t > 0 and t % 8 == 0
    return t


def _silu(x):
    return x * jax.nn.sigmoid(x)


def _head_norm(o):
    mu = jnp.mean(o, axis=-1, keepdims=True)
    d = o - mu
    var = jnp.mean(d * d, axis=-1, keepdims=True)
    return d * lax.rsqrt(var + LN_EPS)


def _ln_store(y, g_ref, b_ref, o_ref, ob_ref):
    mu = jnp.mean(y, axis=-1, keepdims=True)
    d = y - mu
    var = jnp.mean(d * d, axis=-1, keepdims=True)
    out = d * lax.rsqrt(var + LN_EPS) * g_ref[...] + b_ref[...]
    o_ref[...] = out
    ob_ref[...] = out.astype(BF16)


def _any_of(j, values):
    pred = None
    for v in values:
        p = j == v
        pred = p if pred is None else jnp.logical_or(pred, p)
    return pred


def _cast_kernel(x_ref, o_ref):
    o_ref[...] = x_ref[...].astype(o_ref.dtype)


def _to_bf16(x, name="to_bf16"):
    shape = x.shape
    x2 = x.reshape(-1, shape[-1])
    rows, cols = x2.shape
    tr = _row_tile(rows, max(16, (CAST_BLOCK_BYTES // (4 * cols)) // 16 * 16))
    out = pl.pallas_call(
        _cast_kernel,
        out_shape=jax.ShapeDtypeStruct((rows, cols), BF16),
        grid=(rows // tr,),
        in_specs=[pl.BlockSpec((tr, cols), lambda i: (i, 0))],
        out_specs=pl.BlockSpec((tr, cols), lambda i: (i, 0)),
        compiler_params=_cp("parallel"),
        name=name,
    )(x2)
    return out.reshape(shape)


def _proj_kernel(*refs, modes, groups, tn):
    use_rope = any(m[0] for m in modes)
    n_in = 4 if use_rope else 2
    x_ref, w_ref = refs[:2]
    cos_ref, sin_ref = refs[2:4] if use_rope else (None, None)
    o_refs = refs[n_in:n_in + len(groups)]
    j = pl.program_id(2)
    sub = PROJ_SUB_COLS if tn % PROJ_SUB_COLS == 0 else tn

    def store(mode, o_ref):
        do_rope, scale = mode
        xb = x_ref[...].astype(BF16)
        for c0 in range(0, tn, sub):
            acc = _dot(xb, w_ref[:, c0:c0 + sub])
            if not do_rope:
                v = acc if scale == 1.0 else acc * scale
                o_ref[:, c0:c0 + sub] = v.astype(o_ref.dtype)
                continue
            c = cos_ref[...]
            s = sin_ref[...]
            for h in range(sub // HEAD_DIM):
                a = acc[:, h * HEAD_DIM:(h + 1) * HEAD_DIM]
                r = a * c + pltpu.roll(a, HEAD_DIM // 2, 1) * s
                if scale != 1.0:
                    r = r * scale
                o_ref[:, c0 + h * HEAD_DIM:c0 + (h + 1) * HEAD_DIM] = r.astype(o_ref.dtype)

    cases = {}
    for jj, m in enumerate(modes):
        k = next(i for i, (s0, n) in enumerate(groups) if s0 <= jj < s0 + n)
        cases.setdefault((m, k), []).append(jj)
    if len(cases) == 1:
        (m, k), = cases
        store(m, o_refs[k])
        return
    for (m, k), js in sorted(cases.items()):
        pl.when(_any_of(j, js))(functools.partial(store, m, o_refs[k]))


def _proj(x, w, outs, modes=None, tn=1024, rope=None, col0=0, layer=0, name="proj"):
    if w.ndim == 2:
        w = w.reshape(1, *w.shape)
    m, k = x.shape
    nt = sum(n for _, n in outs)
    assert col0 % tn == 0 and col0 + nt * tn <= w.shape[2]
    modes = tuple(modes) if modes is not None else ((False, 1.0),) * nt
    assert len(modes) == nt
    tm = _row_tile(m, 1024)
    t0 = col0 // tn
    n_layers = w.shape[0] if layer is None else 1
    lsel = (lambda g: g) if layer is None else (lambda g: layer)
    in_specs = [pl.BlockSpec((tm, k), lambda g, i, j: (i, 0)),
                pl.BlockSpec((None, k, tn), lambda g, i, j: (lsel(g), 0, t0 + j))]
    args = [x, w]
    if any(mm[0] for mm in modes):
        cos, sin = rope
        p_rows = cos.shape[0]
        assert p_rows % tm == 0
        nblk = p_rows // tm
        in_specs += [pl.BlockSpec((tm, HEAD_DIM), lambda g, i, j: (i % nblk, 0))] * 2
        args += [cos, sin]
    groups, out_shapes, out_specs, start = [], [], [], 0
    for dtype, n in outs:
        groups.append((start, n))
        out_shapes.append(jax.ShapeDtypeStruct((n_layers, m, n * tn), dtype))
        out_specs.append(pl.BlockSpec((None, tm, tn),
                                      lambda g, i, j, s=start, n=n: (g, i, jnp.clip(j - s, 0, n - 1))))
        start += n
    res = pl.pallas_call(
        functools.partial(_proj_kernel, modes=modes, groups=tuple(groups), tn=tn),
        out_shape=tuple(out_shapes),
        grid=(n_layers, m // tm, nt),
        in_specs=in_specs,
        out_specs=tuple(out_specs),
        compiler_params=_cp("parallel", "parallel", "arbitrary"),
        name=name,
    )(*args)
    return res if layer is None else tuple(r.reshape(r.shape[1:]) for r in res)


def _proj_ln_kernel(*refs, n_in, alpha):
    a_refs = refs[:n_in]
    w_refs = refs[n_in:2 * n_in]
    x_ref, g_ref, b_ref, o_ref, ob_ref = refs[2 * n_in:]
    tm = x_ref.shape[0]
    sub = PROJ_LN_SUB_ROWS if tm % PROJ_LN_SUB_ROWS == 0 else tm
    for r0 in range(0, tm, sub):
        rows = slice(r0, r0 + sub)
        acc = None
        for a_ref, w_ref in zip(a_refs, w_refs):
            d = _dot(a_ref[rows, :].astype(BF16), w_ref[...])
            acc = d if acc is None else acc + d
        y = alpha * x_ref[rows, :] + acc
        mu = jnp.mean(y, axis=-1, keepdims=True)
        dev = y - mu
        var = jnp.mean(dev * dev, axis=-1, keepdims=True)
        out = dev * lax.rsqrt(var + LN_EPS) * g_ref[...] + b_ref[...]
        o_ref[rows, :] = out
        ob_ref[rows, :] = out.astype(BF16)


def _proj_ln(a_list, w, x, g, b, alpha, layer=0, name="proj_ln"):
    if w.ndim == 2:
        w = w.reshape(1, *w.shape)
    m, d = x.shape
    tm = _row_tile(m, 512)
    in_specs = [pl.BlockSpec((tm, a.shape[1]), lambda i: (i, 0)) for a in a_list]
    row0 = 0
    for a in a_list:
        rows = a.shape[1]
        assert row0 % rows == 0
        in_specs.append(pl.BlockSpec((None, rows, d), lambda i, r=row0 // rows: (layer, r, 0),
                                     pipeline_mode=pl.Buffered(1)))
        row0 += rows
    assert row0 == w.shape[1]
    in_specs += [pl.BlockSpec((tm, d), lambda i: (i, 0)),
                 pl.BlockSpec((1, d), lambda i: (0, 0)),
                 pl.BlockSpec((1, d), lambda i: (0, 0))]
    return pl.pallas_call(
        functools.partial(_proj_ln_kernel, n_in=len(a_list), alpha=alpha),
        out_shape=(jax.ShapeDtypeStruct((m, d), F32), jax.ShapeDtypeStruct((m, d), BF16)),
        grid=(m // tm,),
        in_specs=in_specs,
        out_specs=(pl.BlockSpec((tm, d), lambda i: (i, 0)), pl.BlockSpec((tm, d), lambda i: (i, 0))),
        compiler_params=_cp("parallel"),
        name=name,
    )(*a_list, *([w] * len(a_list)), x, g.reshape(1, d), b.reshape(1, d))


def _ffn_kernel(xb_ref, x_ref, wg_ref, wu_ref, wd_ref, g_ref, b_ref, o_ref, ob_ref, acc_ref, *, alpha):
    f = pl.program_id(1)

    @pl.when(f == 0)
    def _():
        acc_ref[...] = jnp.zeros_like(acc_ref)

    xb = xb_ref[...]
    tf = wg_ref.shape[1]
    sub = PROJ_SUB_COLS if tf % PROJ_SUB_COLS == 0 else tf
    hid = []
    for c0 in range(0, tf, sub):
        gate = _dot(xb, wg_ref[:, c0:c0 + sub])
        up = _dot(xb, wu_ref[:, c0:c0 + sub])
        hid.append((_silu(gate) * up).astype(BF16))
    hid = jnp.concatenate(hid, axis=1) if len(hid) > 1 else hid[0]
    acc_ref[...] += _dot(hid, wd_ref[...])

    @pl.when(f == pl.num_programs(1) - 1)
    def _():
        _ln_store(alpha * x_ref[...] + acc_ref[...], g_ref, b_ref, o_ref, ob_ref)


def _ffn(xb, x, w_gu, w_down, layer, g, b, alpha, name="ffn"):
    m, d = x.shape
    ff = w_down.shape[1]
    tm = _row_tile(m, 512)
    tf = 512 if ff % 512 == 0 else 256
    assert ff % tf == 0
    nf = ff // tf
    return pl.pallas_call(
        functools.partial(_ffn_kernel, alpha=alpha),
        out_shape=(jax.ShapeDtypeStruct((m, d), F32), jax.ShapeDtypeStruct((m, d), BF16)),
        grid=(m // tm, nf),
        in_specs=[pl.BlockSpec((tm, d), lambda i, f: (i, 0)),
                  pl.BlockSpec((tm, d), lambda i, f: (i, 0)),
                  pl.BlockSpec((None, d, tf), lambda i, f: (layer, 0, f)),
                  pl.BlockSpec((None, d, tf), lambda i, f: (layer, 0, nf + f)),
                  pl.BlockSpec((None, tf, d), lambda i, f: (layer, f, 0)),
                  pl.BlockSpec((1, d), lambda i, f: (0, 0)),
                  pl.BlockSpec((1, d), lambda i, f: (0, 0))],
        out_specs=(pl.BlockSpec((tm, d), lambda i, f: (i, 0)), pl.BlockSpec((tm, d), lambda i, f: (i, 0))),
        scratch_shapes=[pltpu.VMEM((tm, d), F32)],
        compiler_params=_cp("parallel", "arbitrary"),
        name=name,
    )(xb, x, w_gu, w_gu, w_down, g.reshape(1, d), b.reshape(1, d))


def _ret_kernel(q_ref, k_ref, v_ref, g_ref, s0_ref, dmat_ref, qdec_ref, kdec_ref, gc_ref,
                o_ref, sout_ref, s_scr, *, n_grp, chunk):
    c = pl.program_id(1)

    @pl.when(c == 0)
    def _():
        s_scr[...] = s0_ref[...]

    units = [(r, h) for r in range(n_grp) for h in range(RET_HEADS)]
    rows = [slice(r * chunk, (r + 1) * chunk) for r, _ in units]
    cols = [slice(h * HEAD_DIM, (h + 1) * HEAD_DIM) for _, h in units]
    n_u = range(len(units))
    q = [q_ref[rows[u], cols[u]].astype(F32) for u in n_u]
    k = [k_ref[rows[u], cols[u]].astype(F32) for u in n_u]
    v = [v_ref[rows[u], cols[u]].astype(BF16) for u in n_u]
    s_prev = [s_scr[r, h] for r, h in units]
    att = [_dot_nt(q[u].astype(BF16), k[u].astype(BF16)) for u in n_u]
    o_inter = [_dot((q[u] * qdec_ref[units[u][1]]).astype(BF16), s_prev[u].astype(BF16)) for u in n_u]
    upd = [_dot_tn(k[u] * kdec_ref[units[u][1]], v[u]) for u in n_u]
    o = [_dot((att[u] * dmat_ref[units[u][1]]).astype(BF16), v[u]) + o_inter[u] for u in n_u]
    for u, (r, h) in enumerate(units):
        s_scr[r, h] = gc_ref[h] * s_prev[u] + upd[u]
        gate = g_ref[rows[u], cols[u]].astype(F32)
        o_ref[rows[u], cols[u]] = (_head_norm(o[u]) * _silu(gate)).astype(o_ref.dtype)

    @pl.when(c == pl.num_programs(1) - 1)
    def _():
        sout_ref[...] = s_scr[...]


def _retention(arr, s0, n_seq, seq_len, chunk, out_dtype, name="retention"):
    nc = seq_len // chunk
    h_n = RET_HEADS
    width = h_n * HEAD_DIM
    n_grp = 4 if (nc == 1 and chunk <= 16 and n_seq % 4 == 0) else 1
    lg = np.log1p(-np.power(2.0, -5.0 - np.arange(h_n, dtype=np.float64)))[:, None, None]
    i = np.arange(chunk, dtype=np.float64)
    diff = i[:, None] - i[None, :]
    dmat = np.where(diff >= 0, np.exp(lg * np.maximum(diff, 0.0)), 0.0).astype(np.float32)
    qdec = np.broadcast_to(np.exp(lg * (i[None, :, None] + 1.0)), (h_n, chunk, HEAD_DIM)).astype(np.float32)
    kdec = np.broadcast_to(np.exp(lg * (chunk - 1.0 - i[None, :, None])), (h_n, chunk, HEAD_DIM)).astype(np.float32)
    gc = np.broadcast_to(np.exp(lg * chunk), (h_n, 1, HEAD_DIM)).astype(np.float32)

    def sec_spec(sec):
        return pl.BlockSpec((n_grp * chunk, width), lambda b, c: (b * nc + c, sec))

    def table_spec(shape):
        return pl.BlockSpec(shape, lambda b, c: (0, 0, 0))

    state_spec = pl.BlockSpec((n_grp, h_n, HEAD_DIM, HEAD_DIM), lambda b, c: (b, 0, 0, 0))
    return pl.pallas_call(
        functools.partial(_ret_kernel, n_grp=n_grp, chunk=chunk),
        out_shape=(jax.ShapeDtypeStruct((n_seq * seq_len, width), out_dtype),
                   jax.ShapeDtypeStruct(s0.shape, F32)),
        grid=(n_seq // n_grp, nc),
        in_specs=[sec_spec(0), sec_spec(1), sec_spec(2), sec_spec(3), state_spec,
                  table_spec(dmat.shape), table_spec(qdec.shape), table_spec(kdec.shape), table_spec(gc.shape)],
        out_specs=(sec_spec(0), state_spec),
        scratch_shapes=[pltpu.VMEM((n_grp, h_n, HEAD_DIM, HEAD_DIM), F32)],
        compiler_params=_cp("parallel", "arbitrary"),
        name=name,
    )(arr, arr, arr, arr, s0, jnp.asarray(dmat), jnp.asarray(qdec), jnp.asarray(kdec), jnp.asarray(gc))


def _moba_prompt_kernel(q_ref, k_ref, v_ref, o_ref, kmean_scr, kb_scr, vt_scr, *, nb, n_sel, hp):
    qi = pl.program_id(2)
    blk_rows = MOBA_BLOCK
    scale = HEAD_DIM ** -0.5
    heads = range(hp)
    cols = [slice(h * HEAD_DIM, (h + 1) * HEAD_DIM) for h in heads]

    @pl.when(qi == 0)
    def _():
        for n in range(nb):
            rows = slice(n * blk_rows, (n + 1) * blk_rows)
            kf = k_ref[rows, :]
            for h in heads:
                kmean_scr[h, n:n + 1, :] = jnp.mean(kf[:, cols[h]], axis=0, keepdims=True)
            kb_scr[rows, :] = kf.astype(BF16)
            vf = v_ref[rows, :]
            for h in heads:
                vt_scr[h, :, rows] = vf[:, cols[h]].T.astype(BF16)

    key_i = lax.broadcasted_iota(jnp.int32, (blk_rows, blk_rows), 0)
    qry_i = lax.broadcasted_iota(jnp.int32, (blk_rows, blk_rows), 1)
    q = [q_ref[:, cols[h]] for h in heads]
    gates = [_dot_nt(kmean_scr[h].astype(BF16), q[h]) for h in heads]
    blk = lax.broadcasted_iota(jnp.int32, gates[0].shape, 0)
    valid = blk < qi
    selb = []
    for h in heads:
        gate = jnp.where(valid, gates[h], -jnp.inf)
        rank = jnp.zeros(gate.shape, jnp.int32)
        for m in range(nb):
            gm = gate[m:m + 1, :]
            beats = jnp.where(gm > gate, 1, jnp.where(gm == gate, jnp.where(blk > m, 1, 0), 0))
            rank = rank + beats
        selb.append(jnp.where(valid, jnp.where(rank < n_sel, 0.0, NEG), NEG))
    qs = [(q[h].astype(F32) * scale).astype(BF16) for h in heads]

    def attend(n_past):
        n_keys = (n_past + 1) * blk_rows
        s = [_dot_nt(kb_scr[0:n_keys, cols[h]], qs[h]) for h in heads]
        p, l = [], []
        for h in heads:
            parts = [s[h][n * blk_rows:(n + 1) * blk_rows, :] + selb[h][n:n + 1, :] for n in range(n_past)]
            parts.append(jnp.where(key_i <= qry_i, s[h][n_past * blk_rows:, :], NEG))
            sh = jnp.concatenate(parts, axis=0) if n_past else parts[0]
            ph = jnp.exp(sh - jnp.max(sh, axis=0, keepdims=True))
            l.append(jnp.sum(ph, axis=0, keepdims=True))
            p.append(ph.astype(BF16))
        o_t = [_dot(vt_scr[h, :, 0:n_keys], p[h]) for h in heads]
        for h in heads:
            o_ref[:, cols[h]] = (o_t[h] / l[h]).T.astype(o_ref.dtype)

    for n_past in range(nb):
        pl.when(qi == n_past)(functools.partial(attend, n_past))


def _moba_prompt(q_arr, q_col, k_arr, v_arr, n_seq, seq_len, name="moba_prompt"):
    assert seq_len % MOBA_BLOCK == 0
    nb = seq_len // MOBA_BLOCK
    n_sel = min(MOBA_TOPK, nb - 1)
    hp = MOBA_HEADS_PER_STEP
    assert MOBA_HEADS % hp == 0 and q_col % hp == 0
    width = hp * HEAD_DIM
    return pl.pallas_call(
        functools.partial(_moba_prompt_kernel, nb=nb, n_sel=n_sel, hp=hp),
        out_shape=jax.ShapeDtypeStruct((n_seq * seq_len, MOBA_HEADS * HEAD_DIM), BF16),
        grid=(n_seq, MOBA_HEADS // hp, nb),
        in_specs=[pl.BlockSpec((MOBA_BLOCK, width), lambda b, h, i: (b * nb + i, q_col // hp + h)),
                  pl.BlockSpec((seq_len, width), lambda b, h, i: (b, h)),
                  pl.BlockSpec((seq_len, width), lambda b, h, i: (b, h))],
        out_specs=pl.BlockSpec((MOBA_BLOCK, width), lambda b, h, i: (b * nb + i, h)),
        scratch_shapes=[pltpu.VMEM((hp, nb, HEAD_DIM), F32),
                        pltpu.VMEM((seq_len, width), BF16),
                        pltpu.VMEM((hp, HEAD_DIM, seq_len), BF16)],
        compiler_params=_cp("parallel", "parallel", "arbitrary"),
        name=name,
    )(q_arr, k_arr, v_arr)


def _moba_sample_kernel(pt_ref, q_ref, kn_ref, vn_ref, bias_ref, own_bias_ref, *refs, n_pg, n_blk, n_sel):
    k_pages = refs[:n_pg]
    v_pages = refs[n_pg:2 * n_pg]
    o_ref, m_scr, l_scr, acc_scr, km_scr = refs[2 * n_pg:]
    c = pl.program_id(1)
    rows = q_ref.shape[0]
    scale = HEAD_DIM ** -0.5
    ppb = MOBA_BLOCK // PAGE_SIZE
    page_rows = PAGE_SIZE * MOBA_HEADS
    qf = q_ref[...]
    qs = (qf * scale).astype(BF16)
    bias = bias_ref[...]

    blocks = range(n_pg // ppb)
    pages = [range(j * ppb, (j + 1) * ppb) for j in blocks]
    kf = [k_pages[p][...] for p in range(n_pg)]
    s_pg = [_dot_nt(qs, kf[p].astype(BF16)) for p in range(n_pg)]
    m, p_, l = [], [], []
    for j in blocks:
        s = jnp.concatenate([s_pg[p] for p in pages[j]], axis=1) + bias
        mj = jnp.max(s, axis=-1, keepdims=True)
        pj = jnp.exp(s - mj)
        m.append(mj)
        l.append(jnp.sum(pj, axis=-1, keepdims=True))
        p_.append(pj.astype(BF16))
    acc_pg = [_dot(p_[j][:, i * page_rows:(i + 1) * page_rows], v_pages[p][...].astype(BF16))
              for j in blocks for i, p in enumerate(pages[j])]
    for j in blocks:
        acc, ksum = None, None
        for i, p in enumerate(pages[j]):
            d = acc_pg[j * ppb + i]
            acc = d if acc is None else acc + d
            ks = jnp.sum(kf[p].reshape(PAGE_SIZE, MOBA_HEADS, HEAD_DIM), axis=0)
            ksum = ks if ksum is None else ksum + ks
        n_glob = c * (n_pg // ppb) + j
        acc_scr[n_glob] = acc
        m_scr[n_glob] = jnp.broadcast_to(m[j], (rows, HEAD_DIM))
        l_scr[n_glob] = jnp.broadcast_to(l[j], (rows, HEAD_DIM))
        km_scr[n_glob] = ksum * (1.0 / MOBA_BLOCK)

    @pl.when(c == pl.num_programs(1) - 1)
    def _():
        reps = rows // MOBA_HEADS
        gates = [jnp.sum(qf * jnp.concatenate([km_scr[n]] * reps, axis=0), axis=-1, keepdims=True)
                 for n in range(n_blk)]
        s_own = _dot_nt(qs, kn_ref[...].astype(BF16)) + own_bias_ref[...]
        m_tot = jnp.max(s_own, axis=-1, keepdims=True)
        sel = []
        for n in range(n_blk):
            rank = jnp.zeros((rows, 1), jnp.int32)
            for mm in range(n_blk):
                if mm == n:
                    continue
                beats = (gates[mm] >= gates[n]) if mm < n else (gates[mm] > gates[n])
                rank = rank + jnp.where(beats, 1, 0)
            sel.append(rank < n_sel)
            m_tot = jnp.maximum(m_tot, jnp.where(sel[n], m_scr[n][:, :1], NEG))
        p_own = jnp.exp(s_own - m_tot)
        den = jnp.sum(p_own, axis=-1, keepdims=True)
        num = _dot(p_own.astype(BF16), vn_ref[...].astype(BF16))
        for n in range(n_blk):
            w = jnp.where(sel[n], jnp.exp(m_scr[n] - m_tot), 0.0)
            num = num + w * acc_scr[n]
            den = den + w[:, :1] * l_scr[n][:, :1]
        o_ref[...] = num / den


def _moba_sample(q, kn, vn, cache_k, cache_v, page_table, name="moba_sample"):
    bs, n_pages = page_table.shape
    rows = q.shape[0] // bs
    t_n = rows // MOBA_HEADS
    past = n_pages * PAGE_SIZE
    assert past % MOBA_BLOCK == 0 and t_n <= MOBA_BLOCK and rows % SUBLANES == 0
    n_blk = past // MOBA_BLOCK
    n_sel = min(MOBA_TOPK, n_blk)
    ppb = MOBA_BLOCK // PAGE_SIZE
    n_pg = 8 if n_pages % 8 == 0 else ppb
    n_steps = n_pages // n_pg
    page_rows = PAGE_SIZE * MOBA_HEADS
    ck = cache_k.reshape(cache_k.shape[0], page_rows, HEAD_DIM)
    cv = cache_v.reshape(cache_v.shape[0], page_rows, HEAD_DIM)
    r_head = np.arange(rows) % MOBA_HEADS
    c_head = np.arange(ppb * page_rows) % MOBA_HEADS
    bias = np.where(r_head[:, None] == c_head[None, :], 0.0, NEG).astype(np.float32)
    r_tok = np.arange(rows) // MOBA_HEADS
    own_ok = (r_head[:, None] == r_head[None, :]) & (r_tok[None, :] <= r_tok[:, None])
    own_bias = np.where(own_ok, 0.0, NEG).astype(np.float32)

    def page_spec(p):
        return pl.BlockSpec((None, page_rows, HEAD_DIM),
                            lambda b, c, pt: (pt[b * n_pages + c * n_pg + p], 0, 0))

    tok_spec = pl.BlockSpec((rows, HEAD_DIM), lambda b, c, pt: (b, 0))
    part = pltpu.VMEM((n_blk, rows, HEAD_DIM), F32)
    grid_spec = pltpu.PrefetchScalarGridSpec(
        num_scalar_prefetch=1,
        grid=(bs, n_steps),
        in_specs=[tok_spec, tok_spec, tok_spec,
                  pl.BlockSpec(bias.shape, lambda b, c, pt: (0, 0)),
                  pl.BlockSpec(own_bias.shape, lambda b, c, pt: (0, 0))] + [page_spec(p) for p in range(n_pg)] * 2,
        out_specs=tok_spec,
        scratch_shapes=[part, part, part, pltpu.VMEM((n_blk, MOBA_HEADS, HEAD_DIM), F32)],
    )
    return pl.pallas_call(
        functools.partial(_moba_sample_kernel, n_pg=n_pg, n_blk=n_blk, n_sel=n_sel),
        out_shape=jax.ShapeDtypeStruct(q.shape, F32),
        grid_spec=grid_spec,
        compiler_params=_cp("parallel", "arbitrary"),
        name=name,
    )(page_table.reshape(-1), q, kn, vn, jnp.asarray(bias), jnp.asarray(own_bias), *([ck] * n_pg), *([cv] * n_pg))


def _memattn_kernel(q_ref, k_ref, v_ref, o_ref, *, n_grp, q_rows, mem_len):
    d = q_ref.shape[1]
    hd = d // MEM_HEADS
    scale = hd ** -0.5
    for s_i in range(n_grp):
        qr = slice(s_i * q_rows, (s_i + 1) * q_rows)
        mr = slice(s_i * mem_len, (s_i + 1) * mem_len)
        for h in range(MEM_HEADS):
            sl = slice(h * hd, (h + 1) * hd)
            qh = (q_ref[qr, sl].astype(F32) * scale).astype(BF16)
            kh = k_ref[mr, sl].astype(BF16)
            vh = v_ref[mr, sl].astype(BF16)
            s = _dot_nt(qh, kh)
            m = jnp.max(s, axis=-1, keepdims=True)
            p = jnp.exp(s - m)
            l = jnp.sum(p, axis=-1, keepdims=True)
            o_ref[qr, sl] = (_dot(p.astype(BF16), vh) / l).astype(o_ref.dtype)


def _memattn_short_kernel(q_ref, k_ref, v_ref, o_ref, *, n_grp, q_rows, mem_len):
    d = q_ref.shape[1]
    hd = d // MEM_HEADS
    scale = hd ** -0.5
    lane_head = lax.broadcasted_iota(jnp.int32, (q_rows, d), 1) // hd
    seqs = range(n_grp)
    qr = [slice(s_i * q_rows, (s_i + 1) * q_rows) for s_i in seqs]
    mr = [slice(s_i * mem_len, (s_i + 1) * mem_len) for s_i in seqs]
    s = []
    for s_i in seqs:
        q = q_ref[qr[s_i], :].astype(F32) * scale
        q_heads = jnp.concatenate([jnp.where(lane_head == h, q, 0.0) for h in range(MEM_HEADS)], axis=0)
        s.append(_dot_nt(q_heads.astype(BF16), k_ref[mr[s_i], :].astype(BF16)))
    p = [jnp.exp(s[s_i] - jnp.max(s[s_i], axis=-1, keepdims=True)) for s_i in seqs]
    o_all = [_dot(p[s_i].astype(BF16), v_ref[mr[s_i], :].astype(BF16)) for s_i in seqs]
    for s_i in seqs:
        o_n = o_all[s_i] * (1.0 / jnp.sum(p[s_i], axis=-1, keepdims=True))
        for h in range(MEM_HEADS):
            o_ref[qr[s_i], h * hd:(h + 1) * hd] = o_n[h * q_rows:(h + 1) * q_rows, h * hd:(h + 1) * hd].astype(
                o_ref.dtype)


def _memattn(q, mem_k, mem_v, layer, n_seq, seq_len, mem_len, out_dtype, name="memattn"):
    d = q.shape[1]
    tm = _row_tile(seq_len, 512)
    nt = seq_len // tm
    short = nt == 1 and tm * MEM_HEADS <= LANES
    n_grp = 2 if (short and n_seq % 2 == 0) else 1
    n_b = n_seq // n_grp
    kv_spec = pl.BlockSpec((n_grp * mem_len, d), lambda b, i: (layer * n_b + b, 0))
    q_spec = pl.BlockSpec((n_grp * tm, d), lambda b, i: (b * nt + i, 0))
    body = _memattn_short_kernel if short else _memattn_kernel
    return pl.pallas_call(
        functools.partial(body, n_grp=n_grp, q_rows=tm, mem_len=mem_len),
        out_shape=jax.ShapeDtypeStruct(q.shape, out_dtype),
        grid=(n_b, nt),
        in_specs=[q_spec, kv_spec, kv_spec],
        out_specs=q_spec,
        compiler_params=_cp("parallel", "arbitrary"),
        name=name,
    )(q, mem_k, mem_v)


def _gla_gate_kernel(x_ref, wz_ref, wg_ref, b_ref, o_ref):
    z = _dot(x_ref[...].astype(BF16), wz_ref[...])
    g = _dot(z.astype(BF16), wg_ref[...]) + b_ref[...]
    o_ref[...] = (jnp.minimum(g, 0.0) - jnp.log1p(jnp.exp(-jnp.abs(g)))) * (1.0 / GLA_TAU)


def _gla_gate(x, w_z, w_gate, b_gate, name="gla_gate"):
    m, d = x.shape
    n = w_gate.shape[1]
    tm = _row_tile(m, 512)
    return pl.pallas_call(
        _gla_gate_kernel,
        out_shape=jax.ShapeDtypeStruct((m, n), F32),
        grid=(m // tm,),
        in_specs=[pl.BlockSpec((tm, d), lambda i: (i, 0)),
                  pl.BlockSpec(w_z.shape, lambda i: (0, 0)),
                  pl.BlockSpec(w_gate.shape, lambda i: (0, 0)),
                  pl.BlockSpec((1, n), lambda i: (0, 0))],
        out_specs=pl.BlockSpec((tm, n), lambda i: (i, 0)),
        compiler_params=_cp("parallel"),
        name=name,
    )(x, w_z, w_gate, b_gate.reshape(1, n))


def _gla_kernel(q_ref, k_ref, v_ref, r_ref, la_ref, s0_ref, o_ref, sout_ref, s_scr, *, n_grp, chunk):
    c = pl.program_id(1)
    _, h_n, dk, dv = s_scr.shape

    @pl.when(c == 0)
    def _():
        s_scr[...] = s0_ref[...]

    r_i = lax.broadcasted_iota(jnp.int32, (chunk, chunk), 0)
    c_i = lax.broadcasted_iota(jnp.int32, (chunk, chunk), 1)
    causal = c_i <= r_i
    tri = jnp.where(causal, 1.0, 0.0).astype(BF16)
    units = [(r, h) for r in range(n_grp) for h in range(h_n)]
    n_u = range(len(units))
    rows = [slice(r * chunk, (r + 1) * chunk) for r, _ in units]
    ksl = [slice(h * dk, (h + 1) * dk) for _, h in units]
    vsl = [slice(h * dv, (h + 1) * dv) for _, h in units]
    q = [q_ref[rows[u], ksl[u]].astype(F32) for u in n_u]
    k = [k_ref[rows[u], ksl[u]].astype(F32) for u in n_u]
    v = [v_ref[rows[u], vsl[u]].astype(BF16) for u in n_u]
    s_prev = [s_scr[r, h] for r, h in units]
    b = []
    for u in n_u:
        la = la_ref[rows[u], ksl[u]]
        la_hi = la.astype(BF16)
        rem = la - la_hi.astype(F32)
        la_mid = rem.astype(BF16)
        la_lo = (rem - la_mid.astype(F32)).astype(BF16)
        b.append(_dot(tri, la_hi) + _dot(tri, la_mid) + _dot(tri, la_lo))
    b_last = [b[u][chunk - 1:chunk, :] for u in n_u]
    b_mid = [b[u][chunk // 2 - 1:chunk // 2, :] for u in n_u]
    att = [_dot_nt((q[u] * jnp.exp(jnp.minimum(b[u] - b_mid[u], EXP_CLAMP))).astype(BF16),
                   (k[u] * jnp.exp(jnp.minimum(b_mid[u] - b[u], EXP_CLAMP))).astype(BF16)) for u in n_u]
    o_inter = [_dot((q[u] * jnp.exp(b[u])).astype(BF16), s_prev[u].astype(BF16)) for u in n_u]
    upd = [_dot_tn(k[u] * jnp.exp(b_last[u] - b[u]), v[u]) for u in n_u]
    o = [_dot(jnp.where(causal, att[u], 0.0).astype(BF16), v[u]) + o_inter[u] for u in n_u]
    for u, (r, h) in enumerate(units):
        col = jnp.transpose(jnp.broadcast_to(jnp.exp(b_last[u]), (LANES, dk)))
        for jv in range(dv // LANES):
            sl = slice(jv * LANES, (jv + 1) * LANES)
            s_scr[r, h, :, sl] = col * s_prev[u][:, sl] + upd[u][:, sl]
        gate = r_ref[rows[u], vsl[u]].astype(F32)
        o_ref[rows[u], vsl[u]] = (_head_norm(o[u]) * _silu(gate)).astype(o_ref.dtype)

    @pl.when(c == pl.num_programs(1) - 1)
    def _():
        sout_ref[...] = s_scr[...]


def _gla(arr, la, s0, n_seq, seq_len, chunk, out_dtype, name="gla"):
    _, h_n, dk, dv = s0.shape
    nc = seq_len // chunk
    qk_w, v_w = h_n * dk, h_n * dv
    assert (2 * qk_w) % v_w == 0
    n_grp = 2 if (nc == 1 and chunk <= 16 and n_seq % 2 == 0) else 1
    state_spec = pl.BlockSpec((n_grp, h_n, dk, dv), lambda b, c: (b, 0, 0, 0))

    def sec_spec(width, blk):
        return pl.BlockSpec((n_grp * chunk, width), lambda b, c: (b * nc + c, blk))

    return pl.pallas_call(
        functools.partial(_gla_kernel, n_grp=n_grp, chunk=chunk),
        out_shape=(jax.ShapeDtypeStruct((n_seq * seq_len, v_w), out_dtype),
                   jax.ShapeDtypeStruct(s0.shape, F32)),
        grid=(n_seq // n_grp, nc),
        in_specs=[sec_spec(qk_w, 0), sec_spec(qk_w, 1), sec_spec(v_w, 2 * qk_w // v_w),
                  sec_spec(v_w, 2 * qk_w // v_w + 1), sec_spec(qk_w, 0), state_spec],
        out_specs=(sec_spec(v_w, 0), state_spec),
        scratch_shapes=[pltpu.VMEM((n_grp, h_n, dk, dv), F32)],
        compiler_params=_cp("parallel", "arbitrary"),
        name=name,
    )(arr, arr, arr, arr, la, s0)


def _rope_tables(pos):
    half = HEAD_DIM // 2
    inv_freq = jnp.power(ROPE_THETA, -jnp.arange(half, dtype=F32) / half)
    ang = pos.astype(F32)[:, None] * inv_freq[None, :]
    cos, sin = jnp.cos(ang), jnp.sin(ang)
    return jnp.concatenate([cos, cos], axis=-1), jnp.concatenate([-sin, sin], axis=-1)


def _layer_tail(x, xb, layer, mem_k, mem_v, n_seq, seq_len, per_seq_dtype, wts, alpha, tag):
    ln_g, ln_b = wts["ln_g"], wts["ln_b"]
    d = x.shape[1]
    mem_len = mem_k.shape[0] // (wts["depth"] * n_seq)
    q, = _proj(xb, wts["w_mem_q"], [(per_seq_dtype, d // 1024)], layer=layer, name=f"memq_{tag}")
    o = _memattn(q, mem_k, mem_v, layer, n_seq, seq_len, mem_len, per_seq_dtype, name=f"memattn_{tag}")
    x, xb = _proj_ln([o], wts["w_mem_o"], x, ln_g[layer, 1], ln_b[layer, 1], alpha, layer=layer,
                     name=f"memo_{tag}")
    return _ffn(xb, x, wts["w_ffn_gu"], wts["w_ffn_down"], layer, ln_g[layer, 2], ln_b[layer, 2], alpha,
                name=f"ffn_{tag}")


def kernel(x_prompt, x_sample, mem_prompt, state_ret, cache_moba_k, cache_moba_v, page_table, state_gla,
           cache_mem_k, cache_mem_v, w_in_a, w_out_a, w_in_c, w_gate_c, b_gate_c, w_out_c,
           w_mem_q, w_mem_k, w_mem_v, w_mem_o, w_ffn_gu, w_ffn_down, ln_g, ln_b):
    n_b, seq, d = x_prompt.shape
    n_bs, t_n, _ = x_sample.shape
    mem_len = mem_prompt.shape[1]
    depth = w_mem_q.shape[0]
    alpha = float((2 * depth) ** 0.25)
    past = page_table.shape[1] * PAGE_SIZE
    n_p, n_s = n_b * seq, n_bs * t_n
    rw = RET_HEADS * HEAD_DIM
    mw = MOBA_HEADS * HEAD_DIM
    _, gh, dk, dv = state_gla.shape
    qk_w = gh * dk
    tn = 1024
    assert rw == tn and mw == tn and qk_w == tn and d % tn == 0
    k_scale = HEAD_DIM ** -0.5

    wts = {
        "depth": depth, "ln_g": ln_g, "ln_b": ln_b,
        "w_mem_q": _to_bf16(w_mem_q, "cast_w_mem_q"), "w_mem_o": _to_bf16(w_mem_o, "cast_w_mem_o"),
        "w_ffn_gu": _to_bf16(w_ffn_gu, "cast_w_ffn_gu"), "w_ffn_down": _to_bf16(w_ffn_down, "cast_w_ffn_down"),
    }
    w_in_a_b = _to_bf16(w_in_a, "cast_w_in_a")
    w_out_a_b = _to_bf16(w_out_a, "cast_w_out_a")
    w_in_c_b = _to_bf16(w_in_c, "cast_w_in_c")
    w_out_c_b = _to_bf16(w_out_c, "cast_w_out_c")
    w_mem_k_b = _to_bf16(w_mem_k, "cast_w_mem_k")
    w_mem_v_b = _to_bf16(w_mem_v, "cast_w_mem_v")
    w_z = w_in_c[:, 2 * qk_w + 2 * d:].astype(BF16)
    w_gate_b = w_gate_c.astype(BF16)
    n_c_tiles = (2 * qk_w + 2 * d) // tn
    gla_modes = [(False, 1.0)] * n_c_tiles
    gla_modes[1] = (False, float(dk ** -0.5))
    a_modes = [(True, 1.0), (True, k_scale), (False, 1.0), (False, 1.0), (True, 1.0), (True, 1.0), (False, 1.0)]

    rope_p = _rope_tables(jnp.arange(seq, dtype=jnp.int32))
    xp = x_prompt.reshape(n_p, d)
    xpb = _to_bf16(xp, "cast_x_prompt")
    mem_b = _to_bf16(mem_prompt.reshape(n_b * mem_len, d), "cast_mem_prompt")
    mem_k_p, = _proj(mem_b, w_mem_k_b, [(F32, d // tn)], layer=None, name="memk_p")
    mem_v_p, = _proj(mem_b, w_mem_v_b, [(F32, d // tn)], layer=None, name="memv_p")

    pa, mk_p, mv_p = _proj(xpb, w_in_a_b, [(BF16, 5), (F32, 1), (F32, 1)], a_modes, rope=rope_p, name="in_a_p")
    ret0 = jnp.zeros((n_b, RET_HEADS, HEAD_DIM, HEAD_DIM), F32)
    ret_chunk = 256 if seq % 256 == 0 else seq
    ret_o, ret_p = _retention(pa, ret0, n_b, seq, ret_chunk, BF16, name="retention_p")
    moba_o = _moba_prompt(pa, 4 * rw // HEAD_DIM, mk_p, mv_p, n_b, seq)
    x, xb = _proj_ln([ret_o, moba_o], w_out_a_b, xp, ln_g[0, 0], ln_b[0, 0], alpha, name="out_a_p")
    mkp2 = mem_k_p.reshape(depth * n_b * mem_len, d)
    mvp2 = mem_v_p.reshape(depth * n_b * mem_len, d)
    x, xb = _layer_tail(x, xb, 0, mkp2, mvp2, n_b, seq, BF16, wts, alpha, "p0")

    pc, = _proj(xb, w_in_c_b, [(BF16, n_c_tiles)], gla_modes, name="in_c_p")
    la = _gla_gate(xb, w_z, w_gate_b, b_gate_c, name="gla_gate_p")
    gla0 = jnp.zeros((n_b, gh, dk, dv), F32)
    gla_chunk = 128 if seq % 128 == 0 else seq
    gla_o, gla_p = _gla(pc, la, gla0, n_b, seq, gla_chunk, BF16, name="gla_p")
    x, xb = _proj_ln([gla_o], w_out_c_b, x, ln_g[1, 0], ln_b[1, 0], alpha, name="out_c_p")
    y_p, _ = _layer_tail(x, xb, 1, mkp2, mvp2, n_b, seq, BF16, wts, alpha, "p1")

    cos_s, sin_s = _rope_tables(past + jnp.arange(t_n, dtype=jnp.int32))
    rope_s = (jnp.tile(cos_s, (n_bs, 1)), jnp.tile(sin_s, (n_bs, 1)))
    xs = x_sample.reshape(n_s, d)
    sa, = _proj(xs, w_in_a_b, [(F32, 7)], a_modes, rope=rope_s, name="in_a_s")
    ret_os, ret_s = _retention(sa, state_ret, n_bs, t_n, t_n, F32, name="retention_s")
    mq_s = sa[:, 4 * rw:4 * rw + mw]
    mk_s = sa[:, 4 * rw + mw:4 * rw + 2 * mw]
    mv_s = sa[:, 4 * rw + 2 * mw:]
    per_head = (n_s * MOBA_HEADS, HEAD_DIM)
    moba_os = _moba_sample(mq_s.reshape(per_head), mk_s.reshape(per_head), mv_s.reshape(per_head),
                           cache_moba_k, cache_moba_v, page_table).reshape(n_s, mw)
    x, xb = _proj_ln([ret_os, moba_os], w_out_a_b, xs, ln_g[0, 0], ln_b[0, 0], alpha, name="out_a_s")
    cmk = cache_mem_k.reshape(depth * n_bs * mem_len, d)
    cmv = cache_mem_v.reshape(depth * n_bs * mem_len, d)
    x, xb = _layer_tail(x, xb, 0, cmk, cmv, n_bs, t_n, F32, wts, alpha, "s0")
    sc, = _proj(xb, w_in_c_b, [(F32, n_c_tiles)], gla_modes, name="in_c_s")
    la_s = _gla_gate(xb, w_z, w_gate_b, b_gate_c, name="gla_gate_s")
    gla_os, gla_s = _gla(sc, la_s, state_gla, n_bs, t_n, t_n, F32, name="gla_s")
    x, xb = _proj_ln([gla_os], w_out_c_b, x, ln_g[1, 0], ln_b[1, 0], alpha, name="out_c_s")
    y_s, _ = _layer_tail(x, xb, 1, cmk, cmv, n_bs, t_n, F32, wts, alpha, "s1")

    return (y_p.reshape(n_b, seq, d), y_s.reshape(n_bs, t_n, d), ret_p,
            mk_p.reshape(n_b, seq, MOBA_HEADS, HEAD_DIM), mv_p.reshape(n_b, seq, MOBA_HEADS, HEAD_DIM), gla_p,
            mem_k_p.reshape(depth, n_b, mem_len, MEM_HEADS, d // MEM_HEADS),
            mem_v_p.reshape(depth, n_b, mem_len, MEM_HEADS, d // MEM_HEADS),
            ret_s, mk_s.reshape(n_bs, t_n, MOBA_HEADS, HEAD_DIM), mv_s.reshape(n_bs, t_n, MOBA_HEADS, HEAD_DIM),
            gla_s)
```

```python
import functools

import jax
import jax.numpy as jnp
import numpy as np
from jax import lax
from jax.experimental import pallas as pl
from jax.experimental.pallas import tpu as pltpu

F32 = jnp.float32
BF16 = jnp.bfloat16

HEAD_DIM = 128
RET_HEADS = 8
MOBA_HEADS = 8
MOBA_BLOCK = 256
MOBA_TOPK = 3
GLA_HEADS = 4
GLA_GATE_RANK = 16
GLA_TAU = 16.0
MEM_HEADS = 4
PAGE_SIZE = 128
ROPE_THETA = 10000.0
LN_EPS = 1e-5

LANES = 128
SUBLANES = 8
V7X_VMEM_BYTES = 64 * 2**20
VMEM_LIMIT = 52 * 2**20

PROJ_LN_SUB_ROWS = 128
PROJ_SUB_COLS = 256
MOBA_HEADS_PER_STEP = 4
CAST_BLOCK_BYTES = 4 * 2**20

NEG = -1e30
EXP_CLAMP = 80.0


def _cp(*sem):
    return pltpu.CompilerParams(dimension_semantics=sem, vmem_limit_bytes=VMEM_LIMIT)


def _dot(a, b):
    return jnp.dot(a, b, preferred_element_type=F32)


def _dot_nt(a, b):
    return lax.dot_general(a, b, (((1,), (1,)), ((), ())), preferred_element_type=F32)


def _dot_tn(a_f32, b):
    return jnp.dot(a_f32.T.astype(BF16), b, preferred_element_type=F32)


def _row_tile(m, pref):
    t = min(m, pref)
    while m % t:
        t -= 8
    assert t > 0 and t % 8 == 0
    return t


def _silu(x):
    return x * jax.nn.sigmoid(x)


def _head_norm(o):
    mu = jnp.mean(o, axis=-1, keepdims=True)
    d = o - mu
    var = jnp.mean(d * d, axis=-1, keepdims=True)
    return d * lax.rsqrt(var + LN_EPS)


def _ln_store(y, g_ref, b_ref, o_ref, ob_ref):
    mu = jnp.mean(y, axis=-1, keepdims=True)
    d = y - mu
    var = jnp.mean(d * d, axis=-1, keepdims=True)
    out = d * lax.rsqrt(var + LN_EPS) * g_ref[...] + b_ref[...]
    o_ref[...] = out
    ob_ref[...] = out.astype(BF16)


def _any_of(j, values):
    pred = None
    for v in values:
        p = j == v
        pred = p if pred is None else jnp.logical_or(pred, p)
    return pred


def _cast_kernel(x_ref, o_ref):
    o_ref[...] = x_ref[...].astype(o_ref.dtype)


def _to_bf16(x, name="to_bf16"):
    shape = x.shape
    x2 = x.reshape(-1, shape[-1])
    rows, cols = x2.shape
    tr = _row_tile(rows, max(16, (CAST_BLOCK_BYTES // (4 * cols)) // 16 * 16))
    out = pl.pallas_call(
        _cast_kernel,
        out_shape=jax.ShapeDtypeStruct((rows, cols), BF16),
        grid=(rows // tr,),
        in_specs=[pl.BlockSpec((tr, cols), lambda i: (i, 0))],
        out_specs=pl.BlockSpec((tr, cols), lambda i: (i, 0)),
        compiler_params=_cp("parallel"),
        name=name,
    )(x2)
    return out.reshape(shape)


def _proj_kernel(*refs, modes, groups, tn):
    use_rope = any(m[0] for m in modes)
    n_in = 4 if use_rope else 2
    x_ref, w_ref = refs[:2]
    cos_ref, sin_ref = refs[2:4] if use_rope else (None, None)
    o_refs = refs[n_in:n_in + len(groups)]
    j = pl.program_id(2)
    sub = PROJ_SUB_COLS if tn % PROJ_SUB_COLS == 0 else tn

    def store(mode, o_ref):
        do_rope, scale = mode
        xb = x_ref[...].astype(BF16)
        for c0 in range(0, tn, sub):
            acc = _dot(xb, w_ref[:, c0:c0 + sub])
            if not do_rope:
                v = acc if scale == 1.0 else acc * scale
                o_ref[:, c0:c0 + sub] = v.astype(o_ref.dtype)
                continue
            c = cos_ref[...]
            s = sin_ref[...]
            for h in range(sub // HEAD_DIM):
                a = acc[:, h * HEAD_DIM:(h + 1) * HEAD_DIM]
                r = a * c + pltpu.roll(a, HEAD_DIM // 2, 1) * s
                if scale != 1.0:
                    r = r * scale
                o_ref[:, c0 + h * HEAD_DIM:c0 + (h + 1) * HEAD_DIM] = r.astype(o_ref.dtype)

    cases = {}
    for jj, m in enumerate(modes):
        k = next(i for i, (s0, n) in enumerate(groups) if s0 <= jj < s0 + n)
        cases.setdefault((m, k), []).append(jj)
    if len(cases) == 1:
        (m, k), = cases
        store(m, o_refs[k])
        return
    for (m, k), js in sorted(cases.items()):
        pl.when(_any_of(j, js))(functools.partial(store, m, o_refs[k]))


def _proj(x, w, outs, modes=None, tn=1024, rope=None, col0=0, layer=0, name="proj"):
    if w.ndim == 2:
        w = w.reshape(1, *w.shape)
    m, k = x.shape
    nt = sum(n for _, n in outs)
    assert col0 % tn == 0 and col0 + nt * tn <= w.shape[2]
    modes = tuple(modes) if modes is not None else ((False, 1.0),) * nt
    assert len(modes) == nt
    tm = _row_tile(m, 1024)
    t0 = col0 // tn
    n_layers = w.shape[0] if layer is None else 1
    lsel = (lambda g: g) if layer is None else (lambda g: layer)
    in_specs = [pl.BlockSpec((tm, k), lambda g, i, j: (i, 0)),
                pl.BlockSpec((None, k, tn), lambda g, i, j: (lsel(g), 0, t0 + j))]
    args = [x, w]
    if any(mm[0] for mm in modes):
        cos, sin = rope
        p_rows = cos.shape[0]
        assert p_rows % tm == 0
        nblk = p_rows // tm
        in_specs += [pl.BlockSpec((tm, HEAD_DIM), lambda g, i, j: (i % nblk, 0))] * 2
        args += [cos, sin]
    groups, out_shapes, out_specs, start = [], [], [], 0
    for dtype, n in outs:
        groups.append((start, n))
        out_shapes.append(jax.ShapeDtypeStruct((n_layers, m, n * tn), dtype))
        out_specs.append(pl.BlockSpec((None, tm, tn),
                                      lambda g, i, j, s=start, n=n: (g, i, jnp.clip(j - s, 0, n - 1))))
        start += n
    res = pl.pallas_call(
        functools.partial(_proj_kernel, modes=modes, groups=tuple(groups), tn=tn),
        out_shape=tuple(out_shapes),
        grid=(n_layers, m // tm, nt),
        in_specs=in_specs,
        out_specs=tuple(out_specs),
        compiler_params=_cp("parallel", "parallel", "arbitrary"),
        name=name,
    )(*args)
    return res if layer is None else tuple(r.reshape(r.shape[1:]) for r in res)


def _proj_ln_kernel(*refs, n_in, alpha):
    a_refs = refs[:n_in]
    w_refs = refs[n_in:2 * n_in]
    x_ref, g_ref, b_ref, o_ref, ob_ref = refs[2 * n_in:]
    tm = x_ref.shape[0]
    sub = PROJ_LN_SUB_ROWS if tm % PROJ_LN_SUB_ROWS == 0 else tm
    for r0 in range(0, tm, sub):
        rows = slice(r0, r0 + sub)
        acc = None
        for a_ref, w_ref in zip(a_refs, w_refs):
            d = _dot(a_ref[rows, :].astype(BF16), w_ref[...])
            acc = d if acc is None else acc + d
        y = alpha * x_ref[rows, :] + acc
        mu = jnp.mean(y, axis=-1, keepdims=True)
        dev = y - mu
        var = jnp.mean(dev * dev, axis=-1, keepdims=True)
        out = dev * lax.rsqrt(var + LN_EPS) * g_ref[...] + b_ref[...]
        o_ref[rows, :] = out
        ob_ref[rows, :] = out.astype(BF16)


def _proj_ln(a_list, w, x, g, b, alpha, layer=0, name="proj_ln"):
    if w.ndim == 2:
        w = w.reshape(1, *w.shape)
    m, d = x.shape
    tm = _row_tile(m, 512)
    in_specs = [pl.BlockSpec((tm, a.shape[1]), lambda i: (i, 0)) for a in a_list]
    row0 = 0
    for a in a_list:
        rows = a.shape[1]
        assert row0 % rows == 0
        in_specs.append(pl.BlockSpec((None, rows, d), lambda i, r=row0 // rows: (layer, r, 0),
                                     pipeline_mode=pl.Buffered(1)))
        row0 += rows
    assert row0 == w.shape[1]
    in_specs += [pl.BlockSpec((tm, d), lambda i: (i, 0)),
                 pl.BlockSpec((1, d), lambda i: (0, 0)),
                 pl.BlockSpec((1, d), lambda i: (0, 0))]
    return pl.pallas_call(
        functools.partial(_proj_ln_kernel, n_in=len(a_list), alpha=alpha),
        out_shape=(jax.ShapeDtypeStruct((m, d), F32), jax.ShapeDtypeStruct((m, d), BF16)),
        grid=(m // tm,),
        in_specs=in_specs,
        out_specs=(pl.BlockSpec((tm, d), lambda i: (i, 0)), pl.BlockSpec((tm, d), lambda i: (i, 0))),
        compiler_params=_cp("parallel"),
        name=name,
    )(*a_list, *([w] * len(a_list)), x, g.reshape(1, d), b.reshape(1, d))


def _ffn_kernel(xb_ref, x_ref, wg_ref, wu_ref, wd_ref, g_ref, b_ref, o_ref, ob_ref, acc_ref, *, alpha):
    f = pl.program_id(1)

    @pl.when(f == 0)
    def _():
        acc_ref[...] = jnp.zeros_like(acc_ref)

    xb = xb_ref[...]
    tf = wg_ref.shape[1]
    sub = PROJ_SUB_COLS if tf % PROJ_SUB_COLS == 0 else tf
    hid = []
    for c0 in range(0, tf, sub):
        gate = _dot(xb, wg_ref[:, c0:c0 + sub])
        up = _dot(xb, wu_ref[:, c0:c0 + sub])
        hid.append((_silu(gate) * up).astype(BF16))
    hid = jnp.concatenate(hid, axis=1) if len(hid) > 1 else hid[0]
    acc_ref[...] += _dot(hid, wd_ref[...])

    @pl.when(f == pl.num_programs(1) - 1)
    def _():
        _ln_store(alpha * x_ref[...] + acc_ref[...], g_ref, b_ref, o_ref, ob_ref)


def _ffn(xb, x, w_gu, w_down, layer, g, b, alpha, name="ffn"):
    m, d = x.shape
    ff = w_down.shape[1]
    tm = _row_tile(m, 512)
    tf = 512 if ff % 512 == 0 else 256
    assert ff % tf == 0
    nf = ff // tf
    return pl.pallas_call(
        functools.partial(_ffn_kernel, alpha=alpha),
        out_shape=(jax.ShapeDtypeStruct((m, d), F32), jax.ShapeDtypeStruct((m, d), BF16)),
        grid=(m // tm, nf),
        in_specs=[pl.BlockSpec((tm, d), lambda i, f: (i, 0)),
                  pl.BlockSpec((tm, d), lambda i, f: (i, 0)),
                  pl.BlockSpec((None, d, tf), lambda i, f: (layer, 0, f)),
                  pl.BlockSpec((None, d, tf), lambda i, f: (layer, 0, nf + f)),
                  pl.BlockSpec((None, tf, d), lambda i, f: (layer, f, 0)),
                  pl.BlockSpec((1, d), lambda i, f: (0, 0)),
                  pl.BlockSpec((1, d), lambda i, f: (0, 0))],
        out_specs=(pl.BlockSpec((tm, d), lambda i, f: (i, 0)), pl.BlockSpec((tm, d), lambda i, f: (i, 0))),
        scratch_shapes=[pltpu.VMEM((tm, d), F32)],
        compiler_params=_cp("parallel", "arbitrary"),
        name=name,
    )(xb, x, w_gu, w_gu, w_down, g.reshape(1, d), b.reshape(1, d))


def _ret_kernel(q_ref, k_ref, v_ref, g_ref, s0_ref, dmat_ref, qdec_ref, kdec_ref, gc_ref,
                o_ref, sout_ref, s_scr, *, n_grp, chunk):
    c = pl.program_id(1)

    @pl.when(c == 0)
    def _():
        s_scr[...] = s0_ref[...]

    units = [(r, h) for r in range(n_grp) for h in range(RET_HEADS)]
    rows = [slice(r * chunk, (r + 1) * chunk) for r, _ in units]
    cols = [slice(h * HEAD_DIM, (h + 1) * HEAD_DIM) for _, h in units]
    n_u = range(len(units))
    q = [q_ref[rows[u], cols[u]].astype(F32) for u in n_u]
    k = [k_ref[rows[u], cols[u]].astype(F32) for u in n_u]
    v = [v_ref[rows[u], cols[u]].astype(BF16) for u in n_u]
    s_prev = [s_scr[r, h] for r, h in units]
    att = [_dot_nt(q[u].astype(BF16), k[u].astype(BF16)) for u in n_u]
    o_inter = [_dot((q[u] * qdec_ref[units[u][1]]).astype(BF16), s_prev[u].astype(BF16)) for u in n_u]
    upd = [_dot_tn(k[u] * kdec_ref[units[u][1]], v[u]) for u in n_u]
    o = [_dot((att[u] * dmat_ref[units[u][1]]).astype(BF16), v[u]) + o_inter[u] for u in n_u]
    for u, (r, h) in enumerate(units):
        s_scr[r, h] = gc_ref[h] * s_prev[u] + upd[u]
        gate = g_ref[rows[u], cols[u]].astype(F32)
        o_ref[rows[u], cols[u]] = (_head_norm(o[u]) * _silu(gate)).astype(o_ref.dtype)

    @pl.when(c == pl.num_programs(1) - 1)
    def _():
        sout_ref[...] = s_scr[...]


def _retention(arr, s0, n_seq, seq_len, chunk, out_dtype, name="retention"):
    nc = seq_len // chunk
    h_n = RET_HEADS
    width = h_n * HEAD_DIM
    n_grp = 4 if (nc == 1 and chunk <= 16 and n_seq % 4 == 0) else 1
    lg = np.log1p(-np.power(2.0, -5.0 - np.arange(h_n, dtype=np.float64)))[:, None, None]
    i = np.arange(chunk, dtype=np.float64)
    diff = i[:, None] - i[None, :]
    dmat = np.where(diff >= 0, np.exp(lg * np.maximum(diff, 0.0)), 0.0).astype(np.float32)
    qdec = np.broadcast_to(np.exp(lg * (i[None, :, None] + 1.0)), (h_n, chunk, HEAD_DIM)).astype(np.float32)
    kdec = np.broadcast_to(np.exp(lg * (chunk - 1.0 - i[None, :, None])), (h_n, chunk, HEAD_DIM)).astype(np.float32)
    gc = np.broadcast_to(np.exp(lg * chunk), (h_n, 1, HEAD_DIM)).astype(np.float32)

    def sec_spec(sec):
        return pl.BlockSpec((n_grp * chunk, width), lambda b, c: (b * nc + c, sec))

    def table_spec(shape):
        return pl.BlockSpec(shape, lambda b, c: (0, 0, 0))

    state_spec = pl.BlockSpec((n_grp, h_n, HEAD_DIM, HEAD_DIM), lambda b, c: (b, 0, 0, 0))
    return pl.pallas_call(
        functools.partial(_ret_kernel, n_grp=n_grp, chunk=chunk),
        out_shape=(jax.ShapeDtypeStruct((n_seq * seq_len, width), out_dtype),
                   jax.ShapeDtypeStruct(s0.shape, F32)),
        grid=(n_seq // n_grp, nc),
        in_specs=[sec_spec(0), sec_spec(1), sec_spec(2), sec_spec(3), state_spec,
                  table_spec(dmat.shape), table_spec(qdec.shape), table_spec(kdec.shape), table_spec(gc.shape)],
        out_specs=(sec_spec(0), state_spec),
        scratch_shapes=[pltpu.VMEM((n_grp, h_n, HEAD_DIM, HEAD_DIM), F32)],
        compiler_params=_cp("parallel", "arbitrary"),
        name=name,
    )(arr, arr, arr, arr, s0, jnp.asarray(dmat), jnp.asarray(qdec), jnp.asarray(kdec), jnp.asarray(gc))


def _moba_prompt_kernel(q_ref, k_ref, v_ref, o_ref, kmean_scr, kb_scr, vt_scr, *, nb, n_sel, hp):
    qi = pl.program_id(2)
    blk_rows = MOBA_BLOCK
    scale = HEAD_DIM ** -0.5
    heads = range(hp)
    cols = [slice(h * HEAD_DIM, (h + 1) * HEAD_DIM) for h in heads]

    @pl.when(qi == 0)
    def _():
        for n in range(nb):
            rows = slice(n * blk_rows, (n + 1) * blk_rows)
            kf = k_ref[rows, :]
            for h in heads:
                kmean_scr[h, n:n + 1, :] = jnp.mean(kf[:, cols[h]], axis=0, keepdims=True)
            kb_scr[rows, :] = kf.astype(BF16)
            vf = v_ref[rows, :]
            for h in heads:
                vt_scr[h, :, rows] = vf[:, cols[h]].T.astype(BF16)

    key_i = lax.broadcasted_iota(jnp.int32, (blk_rows, blk_rows), 0)
    qry_i = lax.broadcasted_iota(jnp.int32, (blk_rows, blk_rows), 1)
    q = [q_ref[:, cols[h]] for h in heads]
    gates = [_dot_nt(kmean_scr[h].astype(BF16), q[h]) for h in heads]
    blk = lax.broadcasted_iota(jnp.int32, gates[0].shape, 0)
    valid = blk < qi
    selb = []
    for h in heads:
        gate = jnp.where(valid, gates[h], -jnp.inf)
        rank = jnp.zeros(gate.shape, jnp.int32)
        for m in range(nb):
            gm = gate[m:m + 1, :]
            beats = jnp.where(gm > gate, 1, jnp.where(gm == gate, jnp.where(blk > m, 1, 0), 0))
            rank = rank + beats
        selb.append(jnp.where(valid, jnp.where(rank < n_sel, 0.0, NEG), NEG))
    qs = [(q[h].astype(F32) * scale).astype(BF16) for h in heads]

    def attend(n_past):
        n_keys = (n_past + 1) * blk_rows
        s = [_dot_nt(kb_scr[0:n_keys, cols[h]], qs[h]) for h in heads]
        p, l = [], []
        for h in heads:
            parts = [s[h][n * blk_rows:(n + 1) * blk_rows, :] + selb[h][n:n + 1, :] for n in range(n_past)]
            parts.append(jnp.where(key_i <= qry_i, s[h][n_past * blk_rows:, :], NEG))
            sh = jnp.concatenate(parts, axis=0) if n_past else parts[0]
            ph = jnp.exp(sh - jnp.max(sh, axis=0, keepdims=True))
            l.append(jnp.sum(ph, axis=0, keepdims=True))
            p.append(ph.astype(BF16))
        o_t = [_dot(vt_scr[h, :, 0:n_keys], p[h]) for h in heads]
        for h in heads:
            o_ref[:, cols[h]] = (o_t[h] / l[h]).T.astype(o_ref.dtype)

    for n_past in range(nb):
        pl.when(qi == n_past)(functools.partial(attend, n_past))


def _moba_prompt(q_arr, q_col, k_arr, v_arr, n_seq, seq_len, name="moba_prompt"):
    assert seq_len % MOBA_BLOCK == 0
    nb = seq_len // MOBA_BLOCK
    n_sel = min(MOBA_TOPK, nb - 1)
    hp = MOBA_HEADS_PER_STEP
    assert MOBA_HEADS % hp == 0 and q_col % hp == 0
    width = hp * HEAD_DIM
    return pl.pallas_call(
        functools.partial(_moba_prompt_kernel, nb=nb, n_sel=n_sel, hp=hp),
        out_shape=jax.ShapeDtypeStruct((n_seq * seq_len, MOBA_HEADS * HEAD_DIM), BF16),
        grid=(n_seq, MOBA_HEADS // hp, nb),
        in_specs=[pl.BlockSpec((MOBA_BLOCK, width), lambda b, h, i: (b * nb + i, q_col // hp + h)),
                  pl.BlockSpec((seq_len, width), lambda b, h, i: (b, h)),
                  pl.BlockSpec((seq_len, width), lambda b, h, i: (b, h))],
        out_specs=pl.BlockSpec((MOBA_BLOCK, width), lambda b, h, i: (b * nb + i, h)),
        scratch_shapes=[pltpu.VMEM((hp, nb, HEAD_DIM), F32),
                        pltpu.VMEM((seq_len, width), BF16),
                        pltpu.VMEM((hp, HEAD_DIM, seq_len), BF16)],
        compiler_params=_cp("parallel", "parallel", "arbitrary"),
        name=name,
    )(q_arr, k_arr, v_arr)


def _moba_sample_kernel(pt_ref, q_ref, kn_ref, vn_ref, bias_ref, own_bias_ref, *refs, n_pg, n_blk, n_sel):
    k_pages = refs[:n_pg]
    v_pages = refs[n_pg:2 * n_pg]
    o_ref, m_scr, l_scr, acc_scr, km_scr = refs[2 * n_pg:]
    c = pl.program_id(1)
    rows = q_ref.shape[0]
    scale = HEAD_DIM ** -0.5
    ppb = MOBA_BLOCK // PAGE_SIZE
    page_rows = PAGE_SIZE * MOBA_HEADS
    qf = q_ref[...]
    qs = (qf * scale).astype(BF16)
    bias = bias_ref[...]

    blocks = range(n_pg // ppb)
    pages = [range(j * ppb, (j + 1) * ppb) for j in blocks]
    kf = [k_pages[p][...] for p in range(n_pg)]
    s_pg = [_dot_nt(qs, kf[p].astype(BF16)) for p in range(n_pg)]
    m, p_, l = [], [], []
    for j in blocks:
        s = jnp.concatenate([s_pg[p] for p in pages[j]], axis=1) + bias
        mj = jnp.max(s, axis=-1, keepdims=True)
        pj = jnp.exp(s - mj)
        m.append(mj)
        l.append(jnp.sum(pj, axis=-1, keepdims=True))
        p_.append(pj.astype(BF16))
    acc_pg = [_dot(p_[j][:, i * page_rows:(i + 1) * page_rows], v_pages[p][...].astype(BF16))
              for j in blocks for i, p in enumerate(pages[j])]
    for j in blocks:
        acc, ksum = None, None
        for i, p in enumerate(pages[j]):
            d = acc_pg[j * ppb + i]
            acc = d if acc is None else acc + d
            ks = jnp.sum(kf[p].reshape(PAGE_SIZE, MOBA_HEADS, HEAD_DIM), axis=0)
            ksum = ks if ksum is None else ksum + ks
        n_glob = c * (n_pg // ppb) + j
        acc_scr[n_glob] = acc
        m_scr[n_glob] = jnp.broadcast_to(m[j], (rows, HEAD_DIM))
        l_scr[n_glob] = jnp.broadcast_to(l[j], (rows, HEAD_DIM))
        km_scr[n_glob] = ksum * (1.0 / MOBA_BLOCK)

    @pl.when(c == pl.num_programs(1) - 1)
    def _():
        reps = rows // MOBA_HEADS
        gates = [jnp.sum(qf * jnp.concatenate([km_scr[n]] * reps, axis=0), axis=-1, keepdims=True)
                 for n in range(n_blk)]
        s_own = _dot_nt(qs, kn_ref[...].astype(BF16)) + own_bias_ref[...]
        m_tot = jnp.max(s_own, axis=-1, keepdims=True)
        sel = []
        for n in range(n_blk):
            rank = jnp.zeros((rows, 1), jnp.int32)
            for mm in range(n_blk):
                if mm == n:
                    continue
                beats = (gates[mm] >= gates[n]) if mm < n else (gates[mm] > gates[n])
                rank = rank + jnp.where(beats, 1, 0)
            sel.append(rank < n_sel)
            m_tot = jnp.maximum(m_tot, jnp.where(sel[n], m_scr[n][:, :1], NEG))
        p_own = jnp.exp(s_own - m_tot)
        den = jnp.sum(p_own, axis=-1, keepdims=True)
        num = _dot(p_own.astype(BF16), vn_ref[...].astype(BF16))
        for n in range(n_blk):
            w = jnp.where(sel[n], jnp.exp(m_scr[n] - m_tot), 0.0)
            num = num + w * acc_scr[n]
            den = den + w[:, :1] * l_scr[n][:, :1]
        o_ref[...] = num / den


def _moba_sample(q, kn, vn, cache_k, cache_v, page_table, name="moba_sample"):
    bs, n_pages = page_table.shape
    rows = q.shape[0] // bs
    t_n = rows // MOBA_HEADS
    past = n_pages * PAGE_SIZE
    assert past % MOBA_BLOCK == 0 and t_n <= MOBA_BLOCK and rows % SUBLANES == 0
    n_blk = past // MOBA_BLOCK
    n_sel = min(MOBA_TOPK, n_blk)
    ppb = MOBA_BLOCK // PAGE_SIZE
    n_pg = next(c for c in (16, 8, ppb) if n_pages % c == 0)
    n_steps = n_pages // n_pg
    page_rows = PAGE_SIZE * MOBA_HEADS
    ck = cache_k.reshape(cache_k.shape[0], page_rows, HEAD_DIM)
    cv = cache_v.reshape(cache_v.shape[0], page_rows, HEAD_DIM)
    r_head = np.arange(rows) % MOBA_HEADS
    c_head = np.arange(ppb * page_rows) % MOBA_HEADS
    bias = np.where(r_head[:, None] == c_head[None, :], 0.0, NEG).astype(np.float32)
    r_tok = np.arange(rows) // MOBA_HEADS
    own_ok = (r_head[:, None] == r_head[None, :]) & (r_tok[None, :] <= r_tok[:, None])
    own_bias = np.where(own_ok, 0.0, NEG).astype(np.float32)

    def page_spec(p):
        return pl.BlockSpec((None, page_rows, HEAD_DIM),
                            lambda b, c, pt: (pt[b * n_pages + c * n_pg + p], 0, 0))

    tok_spec = pl.BlockSpec((rows, HEAD_DIM), lambda b, c, pt: (b, 0))
    part = pltpu.VMEM((n_blk, rows, HEAD_DIM), F32)
    grid_spec = pltpu.PrefetchScalarGridSpec(
        num_scalar_prefetch=1,
        grid=(bs, n_steps),
        in_specs=[tok_spec, tok_spec, tok_spec,
                  pl.BlockSpec(bias.shape, lambda b, c, pt: (0, 0)),
                  pl.BlockSpec(own_bias.shape, lambda b, c, pt: (0, 0))] + [page_spec(p) for p in range(n_pg)] * 2,
        out_specs=tok_spec,
        scratch_shapes=[part, part, part, pltpu.VMEM((n_blk, MOBA_HEADS, HEAD_DIM), F32)],
    )
    return pl.pallas_call(
        functools.partial(_moba_sample_kernel, n_pg=n_pg, n_blk=n_blk, n_sel=n_sel),
        out_shape=jax.ShapeDtypeStruct(q.shape, F32),
        grid_spec=grid_spec,
        compiler_params=_cp("parallel", "arbitrary"),
        name=name,
    )(page_table.reshape(-1), q, kn, vn, jnp.asarray(bias), jnp.asarray(own_bias), *([ck] * n_pg), *([cv] * n_pg))


def _memattn_kernel(q_ref, k_ref, v_ref, o_ref, *, n_grp, q_rows, mem_len):
    d = q_ref.shape[1]
    hd = d // MEM_HEADS
    scale = hd ** -0.5
    for s_i in range(n_grp):
        qr = slice(s_i * q_rows, (s_i + 1) * q_rows)
        mr = slice(s_i * mem_len, (s_i + 1) * mem_len)
        for h in range(MEM_HEADS):
            sl = slice(h * hd, (h + 1) * hd)
            qh = (q_ref[qr, sl].astype(F32) * scale).astype(BF16)
            kh = k_ref[mr, sl].astype(BF16)
            vh = v_ref[mr, sl].astype(BF16)
            s = _dot_nt(qh, kh)
            m = jnp.max(s, axis=-1, keepdims=True)
            p = jnp.exp(s - m)
            l = jnp.sum(p, axis=-1, keepdims=True)
            o_ref[qr, sl] = (_dot(p.astype(BF16), vh) / l).astype(o_ref.dtype)


def _memattn_short_kernel(q_ref, k_ref, v_ref, o_ref, *, n_grp, q_rows, mem_len):
    d = q_ref.shape[1]
    hd = d // MEM_HEADS
    scale = hd ** -0.5
    lane_head = lax.broadcasted_iota(jnp.int32, (q_rows, d), 1) // hd
    seqs = range(n_grp)
    qr = [slice(s_i * q_rows, (s_i + 1) * q_rows) for s_i in seqs]
    mr = [slice(s_i * mem_len, (s_i + 1) * mem_len) for s_i in seqs]
    s = []
    for s_i in seqs:
        q = q_ref[qr[s_i], :].astype(F32) * scale
        q_heads = jnp.concatenate([jnp.where(lane_head == h, q, 0.0) for h in range(MEM_HEADS)], axis=0)
        s.append(_dot_nt(q_heads.astype(BF16), k_ref[mr[s_i], :].astype(BF16)))
    p = [jnp.exp(s[s_i] - jnp.max(s[s_i], axis=-1, keepdims=True)) for s_i in seqs]
    o_all = [_dot(p[s_i].astype(BF16), v_ref[mr[s_i], :].astype(BF16)) for s_i in seqs]
    for s_i in seqs:
        o_n = o_all[s_i] * (1.0 / jnp.sum(p[s_i], axis=-1, keepdims=True))
        for h in range(MEM_HEADS):
            o_ref[qr[s_i], h * hd:(h + 1) * hd] = o_n[h * q_rows:(h + 1) * q_rows, h * hd:(h + 1) * hd].astype(
                o_ref.dtype)


def _memattn_cache_kernel(q_ref, k_ref, v_ref, o_ref, *, n_grp, q_rows):
    d = q_ref.shape[1]
    hd = d // MEM_HEADS
    mem_len = k_ref.shape[1]
    scale = hd ** -0.5
    seqs = range(n_grp)
    n_q, n_k = MEM_HEADS * q_rows, mem_len * MEM_HEADS
    q_head = lax.broadcasted_iota(jnp.int32, (n_q, n_k), 0) // q_rows
    k_head = lax.broadcasted_iota(jnp.int32, (n_q, n_k), 1) % MEM_HEADS
    bias = jnp.where(q_head == k_head, 0.0, NEG)
    rows = [slice(s_i * q_rows, (s_i + 1) * q_rows) for s_i in seqs]
    s = []
    for s_i in seqs:
        q = q_ref[rows[s_i], :].astype(F32) * scale
        q_st = jnp.concatenate([q[:, h * hd:(h + 1) * hd] for h in range(MEM_HEADS)], axis=0)
        s.append(_dot_nt(q_st.astype(BF16), k_ref[s_i].reshape(n_k, hd).astype(BF16)) + bias)
    p = [jnp.exp(s[s_i] - jnp.max(s[s_i], axis=-1, keepdims=True)) for s_i in seqs]
    o = [_dot(p[s_i].astype(BF16), v_ref[s_i].reshape(n_k, hd).astype(BF16)) for s_i in seqs]
    for s_i in seqs:
        o_n = o[s_i] * (1.0 / jnp.sum(p[s_i], axis=-1, keepdims=True))
        for h in range(MEM_HEADS):
            o_ref[rows[s_i], h * hd:(h + 1) * hd] = o_n[h * q_rows:(h + 1) * q_rows, :].astype(o_ref.dtype)


def _memattn_cache(q, cache_k, cache_v, layer, out_dtype, name="memattn_cache"):
    _, n_seq, mem_len, h_n, hd = cache_k.shape
    t_n = q.shape[0] // n_seq
    n_grp = 2 if n_seq % 2 == 0 else 1
    kv_spec = pl.BlockSpec((None, n_grp, mem_len, h_n, hd), lambda b: (layer, b, 0, 0, 0))
    q_spec = pl.BlockSpec((n_grp * t_n, q.shape[1]), lambda b: (b, 0))
    return pl.pallas_call(
        functools.partial(_memattn_cache_kernel, n_grp=n_grp, q_rows=t_n),
        out_shape=jax.ShapeDtypeStruct(q.shape, out_dtype),
        grid=(n_seq // n_grp,),
        in_specs=[q_spec, kv_spec, kv_spec],
        out_specs=q_spec,
        compiler_params=_cp("parallel"),
        name=name,
    )(q, cache_k, cache_v)


def _memattn(q, mem_k, mem_v, layer, n_seq, seq_len, mem_len, out_dtype, name="memattn"):
    d = q.shape[1]
    tm = _row_tile(seq_len, 512)
    nt = seq_len // tm
    short = nt == 1 and tm * MEM_HEADS <= LANES
    n_grp = 2 if (short and n_seq % 2 == 0) else 1
    n_b = n_seq // n_grp
    kv_spec = pl.BlockSpec((n_grp * mem_len, d), lambda b, i: (layer * n_b + b, 0))
    q_spec = pl.BlockSpec((n_grp * tm, d), lambda b, i: (b * nt + i, 0))
    body = _memattn_short_kernel if short else _memattn_kernel
    return pl.pallas_call(
        functools.partial(body, n_grp=n_grp, q_rows=tm, mem_len=mem_len),
        out_shape=jax.ShapeDtypeStruct(q.shape, out_dtype),
        grid=(n_b, nt),
        in_specs=[q_spec, kv_spec, kv_spec],
        out_specs=q_spec,
        compiler_params=_cp("parallel", "arbitrary"),
        name=name,
    )(q, mem_k, mem_v)


def _gla_gate_kernel(x_ref, wz_ref, wg_ref, b_ref, o_ref):
    z = _dot(x_ref[...].astype(BF16), wz_ref[...])
    g = _dot(z.astype(BF16), wg_ref[...]) + b_ref[...]
    o_ref[...] = (jnp.minimum(g, 0.0) - jnp.log1p(jnp.exp(-jnp.abs(g)))) * (1.0 / GLA_TAU)


def _gla_gate(x, w_z, w_gate, b_gate, name="gla_gate"):
    m, d = x.shape
    n = w_gate.shape[1]
    tm = _row_tile(m, 512)
    return pl.pallas_call(
        _gla_gate_kernel,
        out_shape=jax.ShapeDtypeStruct((m, n), F32),
        grid=(m // tm,),
        in_specs=[pl.BlockSpec((tm, d), lambda i: (i, 0)),
                  pl.BlockSpec(w_z.shape, lambda i: (0, 0)),
                  pl.BlockSpec(w_gate.shape, lambda i: (0, 0)),
                  pl.BlockSpec((1, n), lambda i: (0, 0))],
        out_specs=pl.BlockSpec((tm, n), lambda i: (i, 0)),
        compiler_params=_cp("parallel"),
        name=name,
    )(x, w_z, w_gate, b_gate.reshape(1, n))


def _gla_kernel(q_ref, k_ref, v_ref, r_ref, la_ref, s0_ref, o_ref, sout_ref, s_scr, *, n_grp, chunk):
    c = pl.program_id(1)
    _, h_n, dk, dv = s_scr.shape

    @pl.when(c == 0)
    def _():
        s_scr[...] = s0_ref[...]

    r_i = lax.broadcasted_iota(jnp.int32, (chunk, chunk), 0)
    c_i = lax.broadcasted_iota(jnp.int32, (chunk, chunk), 1)
    causal = c_i <= r_i
    tri = jnp.where(causal, 1.0, 0.0).astype(BF16)
    units = [(r, h) for r in range(n_grp) for h in range(h_n)]
    n_u = range(len(units))
    rows = [slice(r * chunk, (r + 1) * chunk) for r, _ in units]
    ksl = [slice(h * dk, (h + 1) * dk) for _, h in units]
    vsl = [slice(h * dv, (h + 1) * dv) for _, h in units]
    q = [q_ref[rows[u], ksl[u]].astype(F32) for u in n_u]
    k = [k_ref[rows[u], ksl[u]].astype(F32) for u in n_u]
    v = [v_ref[rows[u], vsl[u]].astype(BF16) for u in n_u]
    s_prev = [s_scr[r, h] for r, h in units]
    b = []
    for u in n_u:
        la = la_ref[rows[u], ksl[u]]
        la_hi = la.astype(BF16)
        rem = la - la_hi.astype(F32)
        la_mid = rem.astype(BF16)
        la_lo = (rem - la_mid.astype(F32)).astype(BF16)
        b.append(_dot(tri, la_hi) + _dot(tri, la_mid) + _dot(tri, la_lo))
    b_last = [b[u][chunk - 1:chunk, :] for u in n_u]
    b_mid = [b[u][chunk // 2 - 1:chunk // 2, :] for u in n_u]
    att = [_dot_nt((q[u] * jnp.exp(jnp.minimum(b[u] - b_mid[u], EXP_CLAMP))).astype(BF16),
                   (k[u] * jnp.exp(jnp.minimum(b_mid[u] - b[u], EXP_CLAMP))).astype(BF16)) for u in n_u]
    o_inter = [_dot((q[u] * jnp.exp(b[u])).astype(BF16), s_prev[u].astype(BF16)) for u in n_u]
    upd = [_dot_tn(k[u] * jnp.exp(b_last[u] - b[u]), v[u]) for u in n_u]
    o = [_dot(jnp.where(causal, att[u], 0.0).astype(BF16), v[u]) + o_inter[u] for u in n_u]
    for u, (r, h) in enumerate(units):
        col = jnp.transpose(jnp.broadcast_to(jnp.exp(b_last[u]), (LANES, dk)))
        for jv in range(dv // LANES):
            sl = slice(jv * LANES, (jv + 1) * LANES)
            s_scr[r, h, :, sl] = col * s_prev[u][:, sl] + upd[u][:, sl]
        gate = r_ref[rows[u], vsl[u]].astype(F32)
        o_ref[rows[u], vsl[u]] = (_head_norm(o[u]) * _silu(gate)).astype(o_ref.dtype)

    @pl.when(c == pl.num_programs(1) - 1)
    def _():
        sout_ref[...] = s_scr[...]


def _gla(arr, la, s0, n_seq, seq_len, chunk, out_dtype, name="gla"):
    _, h_n, dk, dv = s0.shape
    nc = seq_len // chunk
    qk_w, v_w = h_n * dk, h_n * dv
    assert (2 * qk_w) % v_w == 0
    n_grp = 2 if (nc == 1 and chunk <= 16 and n_seq % 2 == 0) else 1
    state_spec = pl.BlockSpec((n_grp, h_n, dk, dv), lambda b, c: (b, 0, 0, 0))

    def sec_spec(width, blk):
        return pl.BlockSpec((n_grp * chunk, width), lambda b, c: (b * nc + c, blk))

    return pl.pallas_call(
        functools.partial(_gla_kernel, n_grp=n_grp, chunk=chunk),
        out_shape=(jax.ShapeDtypeStruct((n_seq * seq_len, v_w), out_dtype),
                   jax.ShapeDtypeStruct(s0.shape, F32)),
        grid=(n_seq // n_grp, nc),
        in_specs=[sec_spec(qk_w, 0), sec_spec(qk_w, 1), sec_spec(v_w, 2 * qk_w // v_w),
                  sec_spec(v_w, 2 * qk_w // v_w + 1), sec_spec(qk_w, 0), state_spec],
        out_specs=(sec_spec(v_w, 0), state_spec),
        scratch_shapes=[pltpu.VMEM((n_grp, h_n, dk, dv), F32)],
        compiler_params=_cp("parallel", "arbitrary"),
        name=name,
    )(arr, arr, arr, arr, la, s0)


def _rope_tables(pos):
    half = HEAD_DIM // 2
    inv_freq = jnp.power(ROPE_THETA, -jnp.arange(half, dtype=F32) / half)
    ang = pos.astype(F32)[:, None] * inv_freq[None, :]
    cos, sin = jnp.cos(ang), jnp.sin(ang)
    return jnp.concatenate([cos, cos], axis=-1), jnp.concatenate([-sin, sin], axis=-1)


def _layer_tail(x, xb, layer, mem_k, mem_v, n_seq, seq_len, per_seq_dtype, wts, alpha, tag):
    ln_g, ln_b = wts["ln_g"], wts["ln_b"]
    d = x.shape[1]
    q, = _proj(xb, wts["w_mem_q"], [(per_seq_dtype, d // 1024)], layer=layer, name=f"memq_{tag}")
    if mem_k.ndim == 5:
        o = _memattn_cache(q, mem_k, mem_v, layer, per_seq_dtype, name=f"memattn_{tag}")
    else:
        mem_len = mem_k.shape[0] // (wts["depth"] * n_seq)
        o = _memattn(q, mem_k, mem_v, layer, n_seq, seq_len, mem_len, per_seq_dtype, name=f"memattn_{tag}")
    x, xb = _proj_ln([o], wts["w_mem_o"], x, ln_g[layer, 1], ln_b[layer, 1], alpha, layer=layer,
                     name=f"memo_{tag}")
    return _ffn(xb, x, wts["w_ffn_gu"], wts["w_ffn_down"], layer, ln_g[layer, 2], ln_b[layer, 2], alpha,
                name=f"ffn_{tag}")


def kernel(x_prompt, x_sample, mem_prompt, state_ret, cache_moba_k, cache_moba_v, page_table, state_gla,
           cache_mem_k, cache_mem_v, w_in_a, w_out_a, w_in_c, w_gate_c, b_gate_c, w_out_c,
           w_mem_q, w_mem_k, w_mem_v, w_mem_o, w_ffn_gu, w_ffn_down, ln_g, ln_b):
    n_b, seq, d = x_prompt.shape
    n_bs, t_n, _ = x_sample.shape
    mem_len = mem_prompt.shape[1]
    depth = w_mem_q.shape[0]
    alpha = float((2 * depth) ** 0.25)
    past = page_table.shape[1] * PAGE_SIZE
    n_p, n_s = n_b * seq, n_bs * t_n
    rw = RET_HEADS * HEAD_DIM
    mw = MOBA_HEADS * HEAD_DIM
    _, gh, dk, dv = state_gla.shape
    qk_w = gh * dk
    tn = 1024
    assert rw == tn and mw == tn and qk_w == tn and d % tn == 0
    k_scale = HEAD_DIM ** -0.5

    wts = {
        "depth": depth, "ln_g": ln_g, "ln_b": ln_b,
        "w_mem_q": _to_bf16(w_mem_q, "cast_w_mem_q"), "w_mem_o": _to_bf16(w_mem_o, "cast_w_mem_o"),
        "w_ffn_gu": _to_bf16(w_ffn_gu, "cast_w_ffn_gu"), "w_ffn_down": _to_bf16(w_ffn_down, "cast_w_ffn_down"),
    }
    w_in_a_b = _to_bf16(w_in_a, "cast_w_in_a")
    w_out_a_b = _to_bf16(w_out_a, "cast_w_out_a")
    w_in_c_b = w_in_c[:, :2 * qk_w + 2 * d].astype(BF16)
    w_out_c_b = _to_bf16(w_out_c, "cast_w_out_c")
    w_mem_k_b = _to_bf16(w_mem_k, "cast_w_mem_k")
    w_mem_v_b = _to_bf16(w_mem_v, "cast_w_mem_v")
    w_z = w_in_c[:, 2 * qk_w + 2 * d:].astype(BF16)
    w_gate_b = w_gate_c.astype(BF16)
    n_c_tiles = (2 * qk_w + 2 * d) // tn
    gla_modes = [(False, 1.0)] * n_c_tiles
    gla_modes[1] = (False, float(dk ** -0.5))
    a_modes = [(True, 1.0), (True, k_scale), (False, 1.0), (False, 1.0), (True, 1.0), (True, 1.0), (False, 1.0)]

    rope_p = _rope_tables(jnp.arange(seq, dtype=jnp.int32))
    xp = x_prompt.reshape(n_p, d)
    xpb = _to_bf16(xp, "cast_x_prompt")
    mem_b = _to_bf16(mem_prompt.reshape(n_b * mem_len, d), "cast_mem_prompt")
    mem_k_p, = _proj(mem_b, w_mem_k_b, [(F32, d // tn)], layer=None, name="memk_p")
    mem_v_p, = _proj(mem_b, w_mem_v_b, [(F32, d // tn)], layer=None, name="memv_p")

    pa, mk_p, mv_p = _proj(xpb, w_in_a_b, [(BF16, 5), (F32, 1), (F32, 1)], a_modes, rope=rope_p, name="in_a_p")
    ret0 = jnp.zeros((n_b, RET_HEADS, HEAD_DIM, HEAD_DIM), F32)
    ret_chunk = 256 if seq % 256 == 0 else seq
    ret_o, ret_p = _retention(pa, ret0, n_b, seq, ret_chunk, BF16, name="retention_p")
    moba_o = _moba_prompt(pa, 4 * rw // HEAD_DIM, mk_p, mv_p, n_b, seq)
    x, xb = _proj_ln([ret_o, moba_o], w_out_a_b, xp, ln_g[0, 0], ln_b[0, 0], alpha, name="out_a_p")
    mkp2 = mem_k_p.reshape(depth * n_b * mem_len, d)
    mvp2 = mem_v_p.reshape(depth * n_b * mem_len, d)
    x, xb = _layer_tail(x, xb, 0, mkp2, mvp2, n_b, seq, BF16, wts, alpha, "p0")

    pc, = _proj(xb, w_in_c_b, [(BF16, n_c_tiles)], gla_modes, name="in_c_p")
    la = _gla_gate(xb, w_z, w_gate_b, b_gate_c, name="gla_gate_p")
    gla0 = jnp.zeros((n_b, gh, dk, dv), F32)
    gla_chunk = 128 if seq % 128 == 0 else seq
    gla_o, gla_p = _gla(pc, la, gla0, n_b, seq, gla_chunk, BF16, name="gla_p")
    x, xb = _proj_ln([gla_o], w_out_c_b, x, ln_g[1, 0], ln_b[1, 0], alpha, name="out_c_p")
    y_p, _ = _layer_tail(x, xb, 1, mkp2, mvp2, n_b, seq, BF16, wts, alpha, "p1")

    cos_s, sin_s = _rope_tables(past + jnp.arange(t_n, dtype=jnp.int32))
    rope_s = (jnp.tile(cos_s, (n_bs, 1)), jnp.tile(sin_s, (n_bs, 1)))
    xs = x_sample.reshape(n_s, d)
    sa, = _proj(xs, w_in_a_b, [(F32, 7)], a_modes, rope=rope_s, name="in_a_s")
    ret_os, ret_s = _retention(sa, state_ret, n_bs, t_n, t_n, F32, name="retention_s")
    mq_s = sa[:, 4 * rw:4 * rw + mw]
    mk_s = sa[:, 4 * rw + mw:4 * rw + 2 * mw]
    mv_s = sa[:, 4 * rw + 2 * mw:]
    per_head = (n_s * MOBA_HEADS, HEAD_DIM)
    moba_os = _moba_sample(mq_s.reshape(per_head), mk_s.reshape(per_head), mv_s.reshape(per_head),
                           cache_moba_k, cache_moba_v, page_table).reshape(n_s, mw)
    x, xb = _proj_ln([ret_os, moba_os], w_out_a_b, xs, ln_g[0, 0], ln_b[0, 0], alpha, name="out_a_s")
    cmk, cmv = cache_mem_k, cache_mem_v
    x, xb = _layer_tail(x, xb, 0, cmk, cmv, n_bs, t_n, F32, wts, alpha, "s0")
    sc, = _proj(xb, w_in_c_b, [(F32, n_c_tiles)], gla_modes, name="in_c_s")
    la_s = _gla_gate(xb, w_z, w_gate_b, b_gate_c, name="gla_gate_s")
    gla_os, gla_s = _gla(sc, la_s, state_gla, n_bs, t_n, t_n, F32, name="gla_s")
    x, xb = _proj_ln([gla_os], w_out_c_b, x, ln_g[1, 0], ln_b[1, 0], alpha, name="out_c_s")
    y_s, _ = _layer_tail(x, xb, 1, cmk, cmv, n_bs, t_n, F32, wts, alpha, "s1")

    return (y_p.reshape(n_b, seq, d), y_s.reshape(n_bs, t_n, d), ret_p,
            mk_p.reshape(n_b, seq, MOBA_HEADS, HEAD_DIM), mv_p.reshape(n_b, seq, MOBA_HEADS, HEAD_DIM), gla_p,
            mem_k_p.reshape(depth, n_b, mem_len, MEM_HEADS, d // MEM_HEADS),
            mem_v_p.reshape(depth, n_b, mem_len, MEM_HEADS, d // MEM_HEADS),
            ret_s, mk_s.reshape(n_bs, t_n, MOBA_HEADS, HEAD_DIM), mv_s.reshape(n_bs, t_n, MOBA_HEADS, HEAD_DIM),
            gla_s)
```
